```python
import jax, jax.numpy as jnp
from jax import lax
import numpy as np


D_MODEL = 1024
BATCH = 8
SEQ = 8192
DEPTH = 4

HEAD_DIM = 64
SWA_Q_HEADS = 6
SWA_KV_HEADS = 2
SWA_WINDOW = 128
SB_HEADS = 4
RET_HEADS = 6
RET_CHUNK = 128
BLOCK = 128
ROPE_THETA = 10000.0
D_FF = 4 * D_MODEL
NORM_EPS = 1e-6

SWA_Q_W = SWA_Q_HEADS * HEAD_DIM
SWA_KV_W = SWA_KV_HEADS * HEAD_DIM
SB_W = SB_HEADS * HEAD_DIM
RET_W = RET_HEADS * HEAD_DIM
MIX_W = SWA_Q_W + SB_W + RET_W
IN_W = SWA_Q_W + 2 * SWA_KV_W + 3 * SB_W + 4 * RET_W
IN_SPLITS = np.cumsum([SWA_Q_W, SWA_KV_W, SWA_KV_W, SB_W, SB_W, SB_W, RET_W, RET_W, RET_W]).tolist()

kernel_name = 'hybrid_swa_stickbreak_retention_block'


def rms_norm(x, gain):
    xf = x.astype(jnp.float32)
    y = xf * lax.rsqrt(jnp.mean(xf * xf, axis=-1, keepdims=True) + NORM_EPS)
    return (y * gain.astype(jnp.float32)).astype(x.dtype)


def rope_tables(positions):
    inv_freq = ROPE_THETA ** (-jnp.arange(0, HEAD_DIM, 2, dtype=jnp.float32) / HEAD_DIM)
    ang = positions.astype(jnp.float32)[:, None] * inv_freq[None, :]
    return jnp.cos(ang), jnp.sin(ang)


def apply_rope(x, cos, sin):
    x1, x2 = jnp.split(x.astype(jnp.float32), 2, axis=-1)
    c = cos[None, :, None, :]
    s = sin[None, :, None, :]
    return jnp.concatenate([x1 * c - x2 * s, x1 * s + x2 * c], axis=-1).astype(x.dtype)


def swa_sink_attention(q, k, v, sinks):
    b, s, hq, hd = q.shape
    nb = s // BLOCK
    g = hq // SWA_KV_HEADS
    qb = q.reshape(b, nb, BLOCK, SWA_KV_HEADS, g, hd)

    def band(t):
        tb = t.reshape(b, nb, BLOCK, SWA_KV_HEADS, hd)
        prev = jnp.pad(tb, ((0, 0), (1, 0), (0, 0), (0, 0), (0, 0)))[:, :-1]
        return jnp.concatenate([prev, tb], axis=2)

    kb, vb = band(k), band(v)
    scores = jnp.einsum('bnqhgd,bnkhd->bnhgqk', qb, kb,
                        preferred_element_type=jnp.float32) * (hd ** -0.5)
    qi = jnp.arange(BLOCK)[:, None] + BLOCK
    ki = jnp.arange(2 * BLOCK)[None, :]
    rel = qi - ki
    in_window = (rel >= 0) & (rel < SWA_WINDOW)
    key_abs = jnp.arange(nb)[:, None, None] * BLOCK + ki[None] - BLOCK
    valid = in_window[None] & (key_abs >= 0)
    scores = jnp.where(valid[None, :, None, None], scores, -jnp.inf)
    sink = sinks.astype(jnp.float32).reshape(SWA_KV_HEADS, g)[None, None, :, :, None, None]
    sink = jnp.broadcast_to(sink, scores.shape[:-1] + (1,))
    probs = jax.nn.softmax(jnp.concatenate([scores, sink], axis=-1), axis=-1)[..., :-1]
    out = jnp.einsum('bnhgqk,bnkhd->bnqhgd', probs.astype(v.dtype), vb)
    return out.reshape(b, s, hq * hd)


def stick_breaking_attention(q, k, v):
    b, s, h, hd = q.shape
    nb = s // BLOCK
    qb = jnp.moveaxis(q.reshape(b, nb, BLOCK, h, hd), 1, 0)
    kpos = jnp.arange(s)

    def one_block(args):
        qblk, i = args
        z = jnp.einsum('bqhd,bkhd->bhqk', qblk, k,
                       preferred_element_type=jnp.float32) * (hd ** -0.5)
        qpos = i * BLOCK + jnp.arange(BLOCK)
        strict = kpos[None, :] < qpos[:, None]
        log_beta = jax.nn.log_sigmoid(z)
        log_1m = jnp.where(strict, jax.nn.log_sigmoid(-z), 0.0)
        tail = lax.cumsum(log_1m, axis=3, reverse=True) - log_1m
        w = jnp.where(strict, jnp.exp(log_beta + tail), 0.0)
        return jnp.einsum('bhqk,bkhd->bqhd', w.astype(v.dtype), v)

    out = lax.map(one_block, (qb, jnp.arange(nb)))
    return jnp.moveaxis(out, 0, 1).reshape(b, s, h * hd)


def retention(q, k, v, gate, gn_gain):
    b, s, h, hd = q.shape
    nc = s // RET_CHUNK
    log_gamma = jnp.log1p(-(2.0 ** (-5.0 - jnp.arange(h, dtype=jnp.float32))))
    f = lambda t: t.astype(jnp.float32).reshape(b, nc, RET_CHUNK, h, hd)
    qc, kc, vc = f(q), f(k) * (hd ** -0.5), f(v)
    pos = jnp.arange(RET_CHUNK, dtype=jnp.float32)
    rel = pos[:, None] - pos[None, :]
    decay = jnp.where(rel[None] >= 0,
                      jnp.exp(jnp.maximum(rel, 0.0)[None] * log_gamma[:, None, None]), 0.0)
    intra = jnp.einsum('bnqhd,bnkhd->bnhqk', qc, kc) * decay[None, None]
    o_intra = jnp.einsum('bnhqk,bnkhd->bnqhd', intra, vc)
    k_dec = jnp.exp((RET_CHUNK - 1 - pos)[:, None] * log_gamma[None, :])
    kv = jnp.einsum('bnkhd,bnkhe->nbhde', kc * k_dec[None, None, :, :, None], vc)
    chunk_decay = jnp.exp(RET_CHUNK * log_gamma)[None, :, None, None]

    def step(state, kv_n):
        return chunk_decay * state + kv_n, state

    _, prev_states = lax.scan(step, jnp.zeros((b, h, hd, hd), jnp.float32), kv)
    q_dec = jnp.exp((pos + 1.0)[:, None] * log_gamma[None, :])
    o_cross = jnp.einsum('bnqhd,nbhde->bnqhe', qc * q_dec[None, None, :, :, None], prev_states)
    o = (o_intra + o_cross).reshape(b, s, h, hd)
    mu = jnp.mean(o, axis=-1, keepdims=True)
    var = jnp.mean(jnp.square(o - mu), axis=-1, keepdims=True)
    o = (o - mu) * lax.rsqrt(var + NORM_EPS) * gn_gain.astype(jnp.float32).reshape(h, hd)
    o = jax.nn.silu(gate.astype(jnp.float32)) * o
    return o.reshape(b, s, h * hd).astype(q.dtype)


def hybrid_layer(x, cos, sin, w_in, w_out, sinks, branch_gain, w_up, w_down,
                 g_mix_pre, g_mix_post, g_mlp_pre, g_mlp_post):
    b, s, _ = x.shape
    heads = lambda t, n: t.reshape(b, s, n, HEAD_DIM)
    hn = rms_norm(x, g_mix_pre)
    proj = jnp.einsum('bsd,de->bse', hn, w_in)
    qa, ka, va, qb, kb, vb, qc, kc, vc, gc = jnp.split(proj, IN_SPLITS, axis=-1)
    qa = apply_rope(heads(qa, SWA_Q_HEADS), cos, sin)
    ka = apply_rope(heads(ka, SWA_KV_HEADS), cos, sin)
    out_a = swa_sink_attention(qa, ka, heads(va, SWA_KV_HEADS), sinks)
    out_b = stick_breaking_attention(heads(qb, SB_HEADS), heads(kb, SB_HEADS), heads(vb, SB_HEADS))
    qc = apply_rope(heads(qc, RET_HEADS), cos, sin)
    kc = apply_rope(heads(kc, RET_HEADS), cos, sin)
    ga = branch_gain[:SWA_Q_W]
    gb = branch_gain[SWA_Q_W:SWA_Q_W + SB_W]
    gcn = branch_gain[SWA_Q_W + SB_W:]
    out_c = retention(qc, kc, heads(vc, RET_HEADS), heads(gc, RET_HEADS), gcn)
    mixed = jnp.concatenate([rms_norm(out_a, ga), rms_norm(out_b, gb), out_c], axis=-1)
    y = jnp.einsum('bse,ed->bsd', mixed, w_out)
    x = x + rms_norm(y, g_mix_post)
    hm = jnp.einsum('bsd,df->bsf', rms_norm(x, g_mlp_pre), w_up)
    hm = jnp.square(jax.nn.relu(hm))
    y = jnp.einsum('bsf,fd->bsd', hm, w_down)
    return x + rms_norm(y, g_mlp_post)


def setup_inputs(seed: int = 0) -> dict:
    key = jax.random.key(seed)
    ks = jax.random.split(key, 12)
    nrm = lambda k, shape, scale: jax.random.normal(k, shape, jnp.float32) * scale
    gain = lambda k, shape: 1.0 + 0.05 * jax.random.normal(k, shape, jnp.float32)
    return {
        'x': nrm(ks[0], (BATCH, SEQ, D_MODEL), 1.0),
        'positions': jnp.arange(SEQ, dtype=jnp.int32),
        'w_in': nrm(ks[1], (DEPTH, D_MODEL, IN_W), D_MODEL ** -0.5),
        'w_out': nrm(ks[2], (DEPTH, MIX_W, D_MODEL), MIX_W ** -0.5),
        'sinks': nrm(ks[3], (DEPTH, SWA_Q_HEADS), 0.5),
        'branch_gain': gain(ks[4], (DEPTH, MIX_W)),
        'w_up': nrm(ks[5], (DEPTH, D_MODEL, D_FF), D_MODEL ** -0.5),
        'w_down': nrm(ks[6], (DEPTH, D_FF, D_MODEL), D_FF ** -0.5),
        'norm_mix_pre': gain(ks[7], (DEPTH, D_MODEL)),
        'norm_mix_post': gain(ks[8], (DEPTH, D_MODEL)),
        'norm_mlp_pre': gain(ks[9], (DEPTH, D_MODEL)),
        'norm_mlp_post': gain(ks[10], (DEPTH, D_MODEL)),
    }


def reference(x, positions, w_in, w_out, sinks, branch_gain, w_up, w_down,
              norm_mix_pre, norm_mix_post, norm_mlp_pre, norm_mlp_post):
    cos, sin = rope_tables(positions)
    for layer in range(DEPTH):
        x = hybrid_layer(x, cos, sin, w_in[layer], w_out[layer], sinks[layer], branch_gain[layer],
                         w_up[layer], w_down[layer], norm_mix_pre[layer], norm_mix_post[layer],
                         norm_mlp_pre[layer], norm_mlp_post[layer])
    return x
```

```python
import functools

import numpy as np
import jax
import jax.numpy as jnp
from jax import lax
from jax.experimental import pallas as pl
from jax.experimental.pallas import tpu as pltpu

D_MODEL = 1024
HEAD_DIM = 64
SWA_Q_HEADS = 6
SWA_KV_HEADS = 2
SB_HEADS = 4
RET_HEADS = 6
BLOCK = 128
ROPE_THETA = 10000.0
D_FF = 4 * D_MODEL
NORM_EPS = 1e-6
LANES = 128

SWA_Q_W = SWA_Q_HEADS * HEAD_DIM
SWA_KV_W = SWA_KV_HEADS * HEAD_DIM
SB_W = SB_HEADS * HEAD_DIM
RET_W = RET_HEADS * HEAD_DIM
IN_W = SWA_Q_W + 2 * SWA_KV_W + 3 * SB_W + 4 * RET_W
PROJ_W = IN_W - RET_W

QA_BLK, KA_BLK, VA_BLK = 0, 3, 4
QB_BLK, KB_BLK, VB_BLK = 5, 7, 9
QC_BLK, KC_BLK, VC_BLK = 11, 14, 17
GATE_BLK = 20
N_IN_BLKS = IN_W // LANES
QK_SCALE = HEAD_DIM ** -0.5

SWA_HEAD_ORDER = (0, 3, 1, 4, 2, 5)

SB_DEAD_LOG = -110.0

TOKEN_TILE = 512
VMEM_LIMIT = 56 * 1024 * 1024


def _rms(xf, gain):
    return xf * lax.rsqrt(jnp.mean(xf * xf, axis=-1, keepdims=True) + NORM_EPS) * gain


def _low_half_mask(shape):
    return lax.broadcasted_iota(jnp.int32, shape, len(shape) - 1) < HEAD_DIM


def _stack_heads(pair):
    low = _low_half_mask(pair.shape)
    zero = jnp.zeros_like(pair)
    return jnp.concatenate([jnp.where(low, pair, zero), jnp.where(low, zero, pair)], axis=0)


def _dot_nt(a, b):
    return lax.dot_general(a, b, (((1,), (1,)), ((), ())), preferred_element_type=jnp.float32)


def _dot(a, b):
    return jnp.dot(a, b, preferred_element_type=jnp.float32)


def _in_blk_kind(blk):
    if QA_BLK <= blk < KA_BLK:
        return 1, 1.0
    if blk == KA_BLK:
        return 0, 1.0
    if QB_BLK <= blk < KB_BLK:
        return None, QK_SCALE
    if QC_BLK <= blk < KC_BLK:
        return 0, 1.0
    if KC_BLK <= blk < VC_BLK:
        return 1, 1.0
    return None, 1.0


def _in_proj_kernel(x_ref, g_ref, w_ref, rope_ref, proj_ref, gate_ref):
    hn = _rms(x_ref[...], g_ref[...]).astype(jnp.bfloat16)
    first_half = (lax.broadcasted_iota(jnp.int32, (x_ref.shape[0], LANES), 1) % HEAD_DIM) < HEAD_DIM // 2
    chunk = 2 * LANES
    for c0 in range(0, IN_W, chunk):
        width = min(chunk, IN_W - c0)
        p = _dot(hn, w_ref[:, c0:c0 + width])
        for s0 in range(0, width, LANES):
            blk = (c0 + s0) // LANES
            pb = p[:, s0:s0 + LANES]
            table, scale = _in_blk_kind(blk)
            if table is not None:
                cos = rope_ref[2 * table]
                sin = rope_ref[2 * table + 1]
                rot = jnp.where(first_half, pltpu.roll(pb, LANES - HEAD_DIM // 2, 1),
                                pltpu.roll(pb, HEAD_DIM // 2, 1))
                pb = pb * cos + rot * sin
            elif scale != 1.0:
                pb = pb * scale
            if blk >= GATE_BLK:
                g0 = (blk - GATE_BLK) * LANES
                gate_ref[:, g0:g0 + LANES] = pb
            else:
                proj_ref[:, blk * LANES:(blk + 1) * LANES] = pb.astype(jnp.bfloat16)


def _in_proj(x2d, gain, w_bf16, rope_tab, seq):
    n_tok = x2d.shape[0]
    tm = min(TOKEN_TILE, seq)
    seq_tiles = seq // tm
    const = lambda i: (0, 0)
    return pl.pallas_call(
        _in_proj_kernel,
        grid=(n_tok // tm,),
        in_specs=[
            pl.BlockSpec((tm, D_MODEL), lambda i: (i, 0)),
            pl.BlockSpec((1, D_MODEL), const),
            pl.BlockSpec((D_MODEL, IN_W), const, pipeline_mode=pl.Buffered(1)),
            pl.BlockSpec((4, tm, LANES), lambda i: (0, i % seq_tiles, 0)),
        ],
        out_specs=[
            pl.BlockSpec((tm, PROJ_W), lambda i: (i, 0)),
            pl.BlockSpec((tm, RET_W), lambda i: (i, 0)),
        ],
        out_shape=[
            jax.ShapeDtypeStruct((n_tok, PROJ_W), jnp.bfloat16),
            jax.ShapeDtypeStruct((n_tok, RET_W), jnp.float32),
        ],
        compiler_params=pltpu.CompilerParams(
            dimension_semantics=("arbitrary",), vmem_limit_bytes=VMEM_LIMIT),
        name="in_proj",
    )(x2d, gain, w_bf16, rope_tab)


def _swa_kernel(sink_ref, q_ref, k_ref, kp_ref, v_ref, vp_ref, o_ref):
    i = pl.program_id(1)
    k_own, k_prev = k_ref[0], kp_ref[0]
    v_own, v_prev = v_ref[0], vp_ref[0]
    rows = lax.broadcasted_iota(jnp.int32, (2 * BLOCK, BLOCK), 0)
    cols = lax.broadcasted_iota(jnp.int32, (2 * BLOCK, BLOCK), 1)
    own = cols <= (rows % BLOCK)
    low_rows = lax.broadcasted_iota(jnp.int32, (2 * BLOCK, 1), 0) < BLOCK
    prev_bias = jnp.where(i > 0, 0.0, -jnp.inf).astype(jnp.float32)
    for c in range(SWA_Q_HEADS // 2):
        qs = _stack_heads(q_ref[0, :, c * LANES:(c + 1) * LANES])
        s = jnp.where(own, _dot_nt(qs, k_own), _dot_nt(qs, k_prev) + prev_bias)
        sink = jnp.where(low_rows, sink_ref[SWA_HEAD_ORDER[2 * c]], sink_ref[SWA_HEAD_ORDER[2 * c + 1]])
        m = jnp.maximum(jnp.max(s, axis=-1, keepdims=True), sink)
        p = jnp.exp(s - m)
        denom = jnp.sum(p, axis=-1, keepdims=True) + jnp.exp(sink - m)
        p_own = jnp.where(own, p, 0.0).astype(jnp.bfloat16)
        p_prev = jnp.where(own, 0.0, p).astype(jnp.bfloat16)
        o = (_dot(p_own, v_own) + _dot(p_prev, v_prev)) / denom
        o_ref[0, :, c * LANES:(c + 1) * LANES] = jnp.where(
            _low_half_mask((BLOCK, LANES)), o[:BLOCK], o[BLOCK:])


def _swa(sinks, proj):
    b, s, _ = proj.shape
    prev = lambda bi, i: jnp.maximum(i - 1, 0)
    return pl.pallas_call(
        _swa_kernel,
        grid=(b, s // BLOCK),
        in_specs=[
            pl.BlockSpec(memory_space=pltpu.SMEM),
            pl.BlockSpec((1, BLOCK, SWA_Q_W), lambda bi, i: (bi, i, 0)),
            pl.BlockSpec((1, BLOCK, LANES), lambda bi, i: (bi, i, KA_BLK)),
            pl.BlockSpec((1, BLOCK, LANES), lambda bi, i: (bi, prev(bi, i), KA_BLK)),
            pl.BlockSpec((1, BLOCK, LANES), lambda bi, i: (bi, i, VA_BLK)),
            pl.BlockSpec((1, BLOCK, LANES), lambda bi, i: (bi, prev(bi, i), VA_BLK)),
        ],
        out_specs=pl.BlockSpec((1, BLOCK, SWA_Q_W), lambda bi, i: (bi, i, 0)),
        out_shape=jax.ShapeDtypeStruct((b, s, SWA_Q_W), jnp.float32),
        compiler_params=pltpu.CompilerParams(
            dimension_semantics=("arbitrary", "arbitrary"), vmem_limit_bytes=VMEM_LIMIT),
        name="swa",
    )(sinks, proj, proj, proj, proj, proj)


def _sb_kernel(q_ref, k_ref, v_ref, o_ref):
    i = pl.program_id(2)
    qs = _stack_heads(q_ref[0])
    rows = lax.broadcasted_iota(jnp.int32, (2 * BLOCK, BLOCK), 0)
    cols = lax.broadcasted_iota(jnp.int32, (2 * BLOCK, BLOCK), 1)
    strict = cols < (rows % BLOCK)
    kj = lax.broadcasted_iota(jnp.int32, (BLOCK, 2 * BLOCK), 0)
    ks = lax.broadcasted_iota(jnp.int32, (BLOCK, 2 * BLOCK), 1)
    tail_mat = jnp.where((kj > ks) | (ks >= BLOCK), 1.0, 0.0).astype(jnp.bfloat16)

    def tile(j, carry, acc, diagonal):
        start = pl.multiple_of(j * BLOCK, BLOCK)
        kblk = k_ref[0, pl.ds(start, BLOCK), :]
        vblk = v_ref[0, pl.ds(start, BLOCK), :]
        z = _dot_nt(qs, kblk)
        log_beta = jnp.minimum(z, 0.0) - jnp.log1p(jnp.exp(-jnp.abs(z)))
        log_1m = log_beta - z
        if diagonal:
            log_1m = jnp.where(strict, log_1m, 0.0)
        hi = log_1m.astype(jnp.bfloat16)
        lo = (log_1m - hi.astype(jnp.float32)).astype(jnp.bfloat16)
        sums = _dot(hi, tail_mat) + _dot(lo, tail_mat)
        w = jnp.exp(log_beta + sums[:, :BLOCK] + carry)
        if diagonal:
            w = jnp.where(strict, w, 0.0)
        wb = w.astype(jnp.bfloat16)
        vs = _stack_heads(vblk)
        acc = acc + _dot(wb[:BLOCK], vs[:BLOCK]) + _dot(wb[BLOCK:], vs[BLOCK:])
        return carry + sums[:, BLOCK:], acc

    carry0 = jnp.zeros((2 * BLOCK, BLOCK), jnp.float32)
    acc0 = jnp.zeros((BLOCK, LANES), jnp.float32)
    carry, acc = tile(i, carry0, acc0, True)

    def alive_after(c):
        return (jnp.max(c) > SB_DEAD_LOG).astype(jnp.int32)

    def cond(state):
        j, alive, _, _ = state
        return jnp.logical_and(j >= 0, alive > 0)

    def body(state):
        j, _, c, a = state
        c, a = tile(j, c, a, False)
        return j - 1, alive_after(c), c, a

    _, _, _, acc = lax.while_loop(cond, body, (i - 1, alive_after(carry), carry, acc))
    o_ref[0] = acc


def _sb(proj):
    b, s, _ = proj.shape
    return pl.pallas_call(
        _sb_kernel,
        grid=(b, SB_HEADS // 2, s // BLOCK),
        in_specs=[
            pl.BlockSpec((1, BLOCK, LANES), lambda bi, p, i: (bi, i, QB_BLK + p)),
            pl.BlockSpec((1, s, LANES), lambda bi, p, i: (bi, 0, KB_BLK + p)),
            pl.BlockSpec((1, s, LANES), lambda bi, p, i: (bi, 0, VB_BLK + p)),
        ],
        out_specs=pl.BlockSpec((1, BLOCK, LANES), lambda bi, p, i: (bi, i, p)),
        out_shape=jax.ShapeDtypeStruct((b, s, SB_W), jnp.float32),
        compiler_params=pltpu.CompilerParams(
            dimension_semantics=("arbitrary", "arbitrary", "arbitrary"), vmem_limit_bytes=VMEM_LIMIT),
        name="stick_breaking",
    )(proj, proj, proj)


def _ret_kernel(q_ref, k_ref, v_ref, gate_ref, gain_ref, decay_ref, qdec_ref, kdec_ref, sdec_ref,
                o_ref, state_ref):
    @pl.when(pl.program_id(2) == 0)
    def _():
        state_ref[...] = jnp.zeros_like(state_ref)

    q, k, v = q_ref[0], k_ref[0], v_ref[0]
    intra = (_dot_nt(_stack_heads(q), k) * decay_ref[0]).astype(jnp.bfloat16)
    vs = _stack_heads(v)
    o = _dot(intra[:BLOCK], vs[:BLOCK]) + _dot(intra[BLOCK:], vs[BLOCK:])
    state = state_ref[...]
    qd = (q.astype(jnp.float32) * qdec_ref[0]).astype(jnp.bfloat16)
    o = o + _dot(qd, state.astype(jnp.bfloat16))
    kd = (k.astype(jnp.float32) * kdec_ref[0]).astype(jnp.bfloat16)
    kv = lax.dot_general(kd, v, (((0,), (0,)), ((), ())), preferred_element_type=jnp.float32)
    rows = lax.broadcasted_iota(jnp.int32, (LANES, LANES), 0) < HEAD_DIM
    same_head = rows == _low_half_mask((LANES, LANES))
    state_ref[...] = sdec_ref[0] * state + jnp.where(same_head, kv, 0.0)

    low = _low_half_mask(o.shape)

    def head_mean(t):
        lo_sum = jnp.sum(jnp.where(low, t, 0.0), axis=-1, keepdims=True)
        hi_sum = jnp.sum(jnp.where(low, 0.0, t), axis=-1, keepdims=True)
        return jnp.where(low, lo_sum, hi_sum) * (1.0 / HEAD_DIM)

    cen = o - head_mean(o)
    var = head_mean(cen * cen)
    normed = cen * lax.rsqrt(var + NORM_EPS) * gain_ref[...]
    gate = gate_ref[0]
    o_ref[0] = gate * jax.nn.sigmoid(gate) * normed


def _retention_tables():
    h = jnp.arange(RET_HEADS, dtype=jnp.float32)
    log_gamma = jnp.log1p(-(2.0 ** (-5.0 - h)))
    pos = jnp.arange(BLOCK, dtype=jnp.float32)
    rel = pos[:, None] - pos[None, :]
    decay = jnp.where(rel[None] >= 0,
                      jnp.exp(jnp.maximum(rel, 0.0)[None] * log_gamma[:, None, None]), 0.0)
    decay = decay.reshape(RET_HEADS // 2, 2 * BLOCK, BLOCK)
    per_lane = lambda t: jnp.repeat(t.reshape(-1, RET_HEADS // 2, 2), HEAD_DIM, axis=2)
    to_pairs = lambda t: jnp.moveaxis(per_lane(t).reshape(-1, RET_HEADS // 2, LANES), 1, 0)
    q_dec = to_pairs(jnp.exp((pos + 1.0)[:, None] * log_gamma[None, :]))
    k_dec = to_pairs(jnp.exp((BLOCK - 1 - pos)[:, None] * log_gamma[None, :]))
    chunk_decay = to_pairs(jnp.exp(BLOCK * log_gamma)[None, :])
    s_dec = jnp.broadcast_to(jnp.swapaxes(chunk_decay, 1, 2), (RET_HEADS // 2, LANES, LANES))
    return decay, q_dec, k_dec, s_dec


def _retention(proj, gate, gn_gain):
    b, s, _ = proj.shape
    decay, q_dec, k_dec, s_dec = _retention_tables()
    tab = lambda shape: pl.BlockSpec((1,) + shape, lambda bi, p, i: (p, 0, 0))
    return pl.pallas_call(
        _ret_kernel,
        grid=(b, RET_HEADS // 2, s // BLOCK),
        in_specs=[
            pl.BlockSpec((1, BLOCK, LANES), lambda bi, p, i: (bi, i, QC_BLK + p)),
            pl.BlockSpec((1, BLOCK, LANES), lambda bi, p, i: (bi, i, KC_BLK + p)),
            pl.BlockSpec((1, BLOCK, LANES), lambda bi, p, i: (bi, i, VC_BLK + p)),
            pl.BlockSpec((1, BLOCK, LANES), lambda bi, p, i: (bi, i, p)),
            pl.BlockSpec((1, LANES), lambda bi, p, i: (0, p)),
            tab((2 * BLOCK, BLOCK)), tab((BLOCK, LANES)), tab((BLOCK, LANES)), tab((LANES, LANES)),
        ],
        out_specs=pl.BlockSpec((1, BLOCK, LANES), lambda bi, p, i: (bi, i, p)),
        out_shape=jax.ShapeDtypeStruct((b, s, RET_W), jnp.float32),
        scratch_shapes=[pltpu.VMEM((LANES, LANES), jnp.float32)],
        compiler_params=pltpu.CompilerParams(
            dimension_semantics=("arbitrary", "arbitrary", "arbitrary"), vmem_limit_bytes=VMEM_LIMIT),
        name="retention",
    )(proj, proj, proj, gate, gn_gain, decay, q_dec, k_dec, s_dec)


FF_CHUNK = 512


def _out_mlp_kernel(oa_ref, ob_ref, oc_ref, x_ref, ga_ref, gb_ref, gpost_ref, gpre_ref, gmlp_ref,
                    wout_ref, wup_ref, wdown_ref, o_ref):
    na = _rms(oa_ref[...], ga_ref[...]).astype(jnp.bfloat16)
    nb = _rms(ob_ref[...], gb_ref[...]).astype(jnp.bfloat16)
    nc = oc_ref[...].astype(jnp.bfloat16)
    y = (_dot(na, wout_ref[0:SWA_Q_W, :]) + _dot(nb, wout_ref[SWA_Q_W:SWA_Q_W + SB_W, :])
         + _dot(nc, wout_ref[SWA_Q_W + SB_W:, :]))
    x1 = x_ref[...] + _rms(y, gpost_ref[...])
    hn = _rms(x1, gpre_ref[...]).astype(jnp.bfloat16)
    acc = jnp.zeros_like(x1)
    for f0 in range(0, D_FF, FF_CHUNK):
        u = jnp.maximum(_dot(hn, wup_ref[:, f0:f0 + FF_CHUNK]), 0.0)
        acc = acc + _dot((u * u).astype(jnp.bfloat16), wdown_ref[f0:f0 + FF_CHUNK, :])
    o_ref[...] = x1 + _rms(acc, gmlp_ref[...])


def _out_mlp(oa, ob, oc, x2d, ga, gb, g_post, g_pre, g_mlp, w_out, w_up, w_down):
    n_tok = x2d.shape[0]
    tm = min(TOKEN_TILE, n_tok)
    row = lambda w: pl.BlockSpec((tm, w), lambda i: (i, 0))
    const = lambda shape: pl.BlockSpec(shape, lambda i: (0, 0))
    weight = lambda shape: pl.BlockSpec(shape, lambda i: (0, 0), pipeline_mode=pl.Buffered(1))
    return pl.pallas_call(
        _out_mlp_kernel,
        grid=(n_tok // tm,),
        in_specs=[
            row(SWA_Q_W), row(SB_W), row(RET_W), row(D_MODEL),
            const((1, SWA_Q_W)), const((1, SB_W)), const((1, D_MODEL)), const((1, D_MODEL)),
            const((1, D_MODEL)),
            weight((D_MODEL, D_MODEL)), weight((D_MODEL, D_FF)), weight((D_FF, D_MODEL)),
        ],
        out_specs=row(D_MODEL),
        out_shape=jax.ShapeDtypeStruct((n_tok, D_MODEL), jnp.float32),
        compiler_params=pltpu.CompilerParams(
            dimension_semantics=("arbitrary",), vmem_limit_bytes=VMEM_LIMIT),
        name="out_proj_mlp",
    )(oa, ob, oc, x2d, ga, gb, g_post, g_pre, g_mlp, w_out, w_up, w_down)


def _rope_table(positions):
    inv_freq = ROPE_THETA ** (-jnp.arange(0, HEAD_DIM, 2, dtype=jnp.float32) / HEAD_DIM)
    ang = positions.astype(jnp.float32)[:, None] * inv_freq[None, :]
    cos = jnp.tile(jnp.cos(ang), (1, 4))
    sin = jnp.tile(jnp.concatenate([-jnp.sin(ang), jnp.sin(ang)], axis=1), (1, 2))
    return jnp.stack([cos, sin, cos * QK_SCALE, sin * QK_SCALE])


def _permute_swa_heads(t, axis):
    shape = t.shape
    t = t.reshape(shape[:axis] + (SWA_Q_HEADS, HEAD_DIM) + shape[axis + 1:])
    t = jnp.take(t, jnp.asarray(SWA_HEAD_ORDER), axis=axis)
    return t.reshape(shape)


def kernel(x, positions, w_in, w_out, sinks, branch_gain, w_up, w_down,
           norm_mix_pre, norm_mix_post, norm_mlp_pre, norm_mlp_post):
    b, s, d = x.shape
    depth = w_in.shape[0]
    rope_tab = _rope_table(positions)
    w_in = jnp.concatenate([_permute_swa_heads(w_in[..., :SWA_Q_W], 2), w_in[..., SWA_Q_W:]],
                           axis=-1).astype(jnp.bfloat16)
    w_out = jnp.concatenate([_permute_swa_heads(w_out[:, :SWA_Q_W], 1), w_out[:, SWA_Q_W:]],
                            axis=1).astype(jnp.bfloat16)
    gain_a = _permute_swa_heads(branch_gain[:, :SWA_Q_W], 1)
    w_up = w_up.astype(jnp.bfloat16)
    w_down = w_down.astype(jnp.bfloat16)
    row = lambda t: t.reshape(1, -1)

    x2d = x.reshape(b * s, d)
    for l in range(depth):
        proj, gate = _in_proj(x2d, row(norm_mix_pre[l]), w_in[l], rope_tab, s)
        proj = proj.reshape(b, s, PROJ_W)
        out_a = _swa(sinks[l], proj)
        out_b = _sb(proj)
        out_c = _retention(proj, gate.reshape(b, s, RET_W), row(branch_gain[l, SWA_Q_W + SB_W:]))
        x2d = _out_mlp(out_a.reshape(b * s, SWA_Q_W), out_b.reshape(b * s, SB_W),
                       out_c.reshape(b * s, RET_W), x2d,
                       row(gain_a[l]), row(branch_gain[l, SWA_Q_W:SWA_Q_W + SB_W]),
                       row(norm_mix_post[l]), row(norm_mlp_pre[l]), row(norm_mlp_post[l]),
                       w_out[l], w_up[l], w_down[l])
    return x2d.reshape(b, s, d)
```

```python
import numpy as np
import jax
import jax.numpy as jnp
from jax import lax
from jax.experimental import pallas as pl
from jax.experimental.pallas import tpu as pltpu

D_MODEL = 1024
HEAD_DIM = 64
SWA_Q_HEADS = 6
SWA_KV_HEADS = 2
SB_HEADS = 4
RET_HEADS = 6
BLOCK = 128
ROPE_THETA = 10000.0
D_FF = 4 * D_MODEL
NORM_EPS = 1e-6
LANES = 128

SWA_Q_W = SWA_Q_HEADS * HEAD_DIM
SWA_KV_W = SWA_KV_HEADS * HEAD_DIM
SB_W = SB_HEADS * HEAD_DIM
RET_W = RET_HEADS * HEAD_DIM
IN_W = SWA_Q_W + 2 * SWA_KV_W + 3 * SB_W + 4 * RET_W
PROJ_W = IN_W - RET_W

QA_BLK = 0
QC_BLK, KC_BLK, VC_BLK = 3, 6, 9
KA_BLK, VA_BLK = 12, 13
QB_BLK, KB_BLK, VB_BLK = 14, 16, 18
GATE_BLK = 20
QK_SCALE = HEAD_DIM ** -0.5

SWA_HEAD_ORDER = (0, 3, 1, 4, 2, 5)

SB_DEAD_SUM = 110.0
SB_FUSED_BLOCKS = 3

TOKEN_TILE = 512
VMEM_LIMIT = 56 * 1024 * 1024


def _rms(xf, gain):
    return xf * lax.rsqrt(jnp.mean(xf * xf, axis=-1, keepdims=True) + NORM_EPS) * gain


def _low_half_mask(shape):
    return lax.broadcasted_iota(jnp.int32, shape, len(shape) - 1) < HEAD_DIM


def _stack_heads(pair):
    low = _low_half_mask(pair.shape)
    zero = jnp.zeros_like(pair)
    return jnp.concatenate([jnp.where(low, pair, zero), jnp.where(low, zero, pair)], axis=0)


def _dot_nt(a, b):
    return lax.dot_general(a, b, (((1,), (1,)), ((), ())), preferred_element_type=jnp.float32)


def _dot(a, b):
    return jnp.dot(a, b, preferred_element_type=jnp.float32)


def _in_blk_kind(blk):
    if QA_BLK <= blk < QC_BLK or KC_BLK <= blk < VC_BLK:
        return 1, 1.0
    if QC_BLK <= blk < KC_BLK or blk == KA_BLK:
        return 0, 1.0
    if QB_BLK <= blk < KB_BLK:
        return None, QK_SCALE
    return None, 1.0


def _in_proj_kernel(x_ref, g_ref, w_ref, rope_ref, proj_ref, gate_ref):
    hn = _rms(x_ref[...], g_ref[...]).astype(jnp.bfloat16)
    first_half = (lax.broadcasted_iota(jnp.int32, (x_ref.shape[0], LANES), 1) % HEAD_DIM) < HEAD_DIM // 2
    chunk = 2 * LANES
    for c0 in range(0, IN_W, chunk):
        width = min(chunk, IN_W - c0)
        p = _dot(hn, w_ref[:, c0:c0 + width])
        for s0 in range(0, width, LANES):
            blk = (c0 + s0) // LANES
            pb = p[:, s0:s0 + LANES]
            table, scale = _in_blk_kind(blk)
            if table is not None:
                cos = rope_ref[2 * table]
                sin = rope_ref[2 * table + 1]
                rot = jnp.where(first_half, pltpu.roll(pb, LANES - HEAD_DIM // 2, 1),
                                pltpu.roll(pb, HEAD_DIM // 2, 1))
                pb = pb * cos + rot * sin
            elif scale != 1.0:
                pb = pb * scale
            if blk >= GATE_BLK:
                g0 = (blk - GATE_BLK) * LANES
                gate_ref[:, g0:g0 + LANES] = pb
            else:
                proj_ref[:, blk * LANES:(blk + 1) * LANES] = pb.astype(jnp.bfloat16)


def _in_proj(x2d, gain, w_bf16, rope_tab, seq):
    n_tok = x2d.shape[0]
    tm = min(TOKEN_TILE, seq)
    seq_tiles = seq // tm
    const = lambda i: (0, 0)
    return pl.pallas_call(
        _in_proj_kernel,
        grid=(n_tok // tm,),
        in_specs=[
            pl.BlockSpec((tm, D_MODEL), lambda i: (i, 0)),
            pl.BlockSpec((1, D_MODEL), const),
            pl.BlockSpec((D_MODEL, IN_W), const, pipeline_mode=pl.Buffered(1)),
            pl.BlockSpec((4, tm, LANES), lambda i: (0, i % seq_tiles, 0)),
        ],
        out_specs=[
            pl.BlockSpec((tm, PROJ_W), lambda i: (i, 0)),
            pl.BlockSpec((tm, RET_W), lambda i: (i, 0)),
        ],
        out_shape=[
            jax.ShapeDtypeStruct((n_tok, PROJ_W), jnp.bfloat16),
            jax.ShapeDtypeStruct((n_tok, RET_W), jnp.float32),
        ],
        compiler_params=pltpu.CompilerParams(
            dimension_semantics=("arbitrary",), vmem_limit_bytes=VMEM_LIMIT),
        name="in_proj",
    )(x2d, gain, w_bf16, rope_tab)


def _swa_kernel(sink_ref, q_ref, k_ref, kp_ref, v_ref, vp_ref, o_ref):
    i = pl.program_id(1)
    n_blocks = q_ref.shape[1] // BLOCK
    rows = lax.broadcasted_iota(jnp.int32, (BLOCK, 2 * BLOCK), 0)
    cols = lax.broadcasted_iota(jnp.int32, (BLOCK, 2 * BLOCK), 1)
    own = (cols % BLOCK) <= rows
    low_lanes = _low_half_mask((BLOCK, LANES))
    first_bias = jnp.where(i > 0, 0.0, -jnp.inf).astype(jnp.float32)
    ks = [_stack_heads(kp_ref[0])] + [_stack_heads(k_ref[0, n * BLOCK:(n + 1) * BLOCK, :]) for n in range(n_blocks)]
    vs = [_stack_heads(vp_ref[0])] + [_stack_heads(v_ref[0, n * BLOCK:(n + 1) * BLOCK, :]) for n in range(n_blocks)]
    tiles = [(n, c) for n in range(n_blocks) for c in range(SWA_Q_HEADS // 2)]
    scores = {}
    for n, c in tiles:
        q = q_ref[0, n * BLOCK:(n + 1) * BLOCK, c * LANES:(c + 1) * LANES]
        s_prev = _dot_nt(q, ks[n])
        if n == 0:
            s_prev = s_prev + first_bias
        scores[n, c] = jnp.where(own, _dot_nt(q, ks[n + 1]), s_prev)
    weights = {}
    for n, c in tiles:
        probs, scales = [], []
        for half, head in enumerate(SWA_HEAD_ORDER[2 * c:2 * c + 2]):
            sh = scores[n, c][:, half * BLOCK:(half + 1) * BLOCK]
            sink = sink_ref[head]
            m = jnp.maximum(jnp.max(sh, axis=-1, keepdims=True), sink)
            p = jnp.exp(sh - m)
            probs.append(p)
            scales.append(1.0 / (jnp.sum(p, axis=-1, keepdims=True) + jnp.exp(sink - m)))
        p = jnp.concatenate(probs, axis=1)
        weights[n, c] = (jnp.where(own, p, 0.0).astype(jnp.bfloat16), jnp.where(own, 0.0, p).astype(jnp.bfloat16),
                         jnp.where(low_lanes, scales[0], scales[1]))
    for n, c in tiles:
        p_own, p_prev, scale = weights[n, c]
        o = _dot(p_own, vs[n + 1]) + _dot(p_prev, vs[n])
        o_ref[0, n * BLOCK:(n + 1) * BLOCK, c * LANES:(c + 1) * LANES] = o * scale


def _swa(sinks, proj):
    b, s, _ = proj.shape
    qt = min(TOKEN_TILE, s)
    per = qt // BLOCK
    prev = lambda i: jnp.maximum(i * per - 1, 0)
    return pl.pallas_call(
        _swa_kernel,
        grid=(b, s // qt),
        in_specs=[
            pl.BlockSpec(memory_space=pltpu.SMEM),
            pl.BlockSpec((1, qt, SWA_Q_W), lambda bi, i: (bi, i, QA_BLK)),
            pl.BlockSpec((1, qt, LANES), lambda bi, i: (bi, i, KA_BLK)),
            pl.BlockSpec((1, BLOCK, LANES), lambda bi, i: (bi, prev(i), KA_BLK)),
            pl.BlockSpec((1, qt, LANES), lambda bi, i: (bi, i, VA_BLK)),
            pl.BlockSpec((1, BLOCK, LANES), lambda bi, i: (bi, prev(i), VA_BLK)),
        ],
        out_specs=pl.BlockSpec((1, qt, SWA_Q_W), lambda bi, i: (bi, i, 0)),
        out_shape=jax.ShapeDtypeStruct((b, s, SWA_Q_W), jnp.float32),
        compiler_params=pltpu.CompilerParams(
            dimension_semantics=("arbitrary", "arbitrary"), vmem_limit_bytes=VMEM_LIMIT),
        name="swa",
    )(sinks, proj, proj, proj, proj, proj)


def _sb_kernel(q_ref, k_ref, v_ref, o_ref):
    i = pl.program_id(1)
    pairs = SB_HEADS // 2
    wide = 2 * BLOCK
    q = [q_ref[0, :, p * LANES:(p + 1) * LANES] for p in range(pairs)]
    rows = lax.broadcasted_iota(jnp.int32, (BLOCK, wide), 0)
    cols = lax.broadcasted_iota(jnp.int32, (BLOCK, wide), 1)
    strict = (cols % BLOCK) < rows
    kj = lax.broadcasted_iota(jnp.int32, (wide, wide), 0)
    ks = lax.broadcasted_iota(jnp.int32, (wide, wide), 1)
    tail_mat = jnp.where(((kj < BLOCK) == (ks < BLOCK)) & (kj >= ks), 1.0, 0.0).astype(jnp.bfloat16)

    def load(ref, p, j):
        start = pl.multiple_of(j * BLOCK, BLOCK)
        return _stack_heads(ref[0, pl.ds(start, BLOCK), p * LANES:(p + 1) * LANES])

    def both_heads(col_lo, col_hi):
        return jnp.concatenate([jnp.broadcast_to(col_lo, (BLOCK, BLOCK)),
                                jnp.broadcast_to(col_hi, (BLOCK, BLOCK))], axis=1)

    def tiles(blocks, carry, acc):
        n = len(blocks)
        k_cat = [jnp.concatenate([load(k_ref, p, j) for j, _ in blocks], axis=0) for p in range(pairs)]
        v_cat = [jnp.concatenate([load(v_ref, p, j) for j, _ in blocks], axis=0) for p in range(pairs)]
        z_cat = [_dot_nt(q[p], k_cat[p]) for p in range(pairs)]
        zs, splits = {}, {}
        for p in range(pairs):
            for t, (_, mask) in enumerate(blocks):
                z = z_cat[p][:, t * wide:(t + 1) * wide]
                if isinstance(mask, str):
                    z = jnp.where(strict, z, -jnp.inf)
                elif mask is not None:
                    z = z + mask
                sp = jnp.maximum(z, 0.0) + jnp.log(1.0 + jnp.exp(-jnp.abs(z)))
                hi = sp.astype(jnp.bfloat16)
                lo = (sp - hi.astype(jnp.float32)).astype(jnp.bfloat16)
                zs[p, t] = z
                splits[p, t] = (sp, hi, lo)
        incl = {key: _dot(hi, tail_mat) + _dot(lo, tail_mat) for key, (_, hi, lo) in splits.items()}
        new_carry, new_acc = [], []
        for p in range(pairs):
            c = carry[p]
            ws = []
            for t in range(n):
                sp = splits[p, t][0]
                ws.append(jnp.exp(zs[p, t] - incl[p, t] - c).astype(jnp.bfloat16))
                c = c + both_heads(jnp.sum(sp[:, :BLOCK], axis=-1, keepdims=True),
                                   jnp.sum(sp[:, BLOCK:], axis=-1, keepdims=True))
            new_carry.append(c)
            new_acc.append(acc[p] + _dot(jnp.concatenate(ws, axis=1), v_cat[p]))
        return tuple(new_carry), tuple(new_acc)

    carry = tuple(jnp.zeros((BLOCK, wide), jnp.float32) for _ in range(pairs))
    acc = tuple(jnp.zeros((BLOCK, LANES), jnp.float32) for _ in range(pairs))
    fused = [(i, "diag")] + [
        (jnp.maximum(i - back, 0), jnp.where(i >= back, 0.0, -jnp.inf).astype(jnp.float32))
        for back in range(1, SB_FUSED_BLOCKS)]
    carry, acc = tiles(fused, carry, acc)

    def alive_after(cs):
        return (jnp.min(jnp.minimum(cs[0], cs[1])) < SB_DEAD_SUM).astype(jnp.int32)

    def cond(state):
        j, alive, _, _ = state
        return jnp.logical_and(j >= 0, alive > 0)

    def body(state):
        j, _, cs, accs = state
        cs, accs = tiles([(j, None)], cs, accs)
        return j - 1, alive_after(cs), cs, accs

    _, _, _, acc = lax.while_loop(cond, body, (i - SB_FUSED_BLOCKS, alive_after(carry), carry, acc))
    for p in range(pairs):
        o_ref[0, :, p * LANES:(p + 1) * LANES] = acc[p]


def _sb(proj):
    b, s, _ = proj.shape
    return pl.pallas_call(
        _sb_kernel,
        grid=(b, s // BLOCK),
        in_specs=[
            pl.BlockSpec((1, BLOCK, SB_W), lambda bi, i: (bi, i, QB_BLK * LANES // SB_W)),
            pl.BlockSpec((1, s, SB_W), lambda bi, i: (bi, 0, KB_BLK * LANES // SB_W)),
            pl.BlockSpec((1, s, SB_W), lambda bi, i: (bi, 0, VB_BLK * LANES // SB_W)),
        ],
        out_specs=pl.BlockSpec((1, BLOCK, SB_W), lambda bi, i: (bi, i, 0)),
        out_shape=jax.ShapeDtypeStruct((b, s, SB_W), jnp.float32),
        compiler_params=pltpu.CompilerParams(
            dimension_semantics=("arbitrary", "arbitrary"), vmem_limit_bytes=VMEM_LIMIT),
        name="stick_breaking",
    )(proj, proj, proj)


def _ret_kernel(q_ref, k_ref, v_ref, gate_ref, gain_ref, decay_ref, qdec_ref, kdec_ref, sdec_ref,
                o_ref, state_ref):
    @pl.when(pl.program_id(1) == 0)
    def _():
        state_ref[...] = jnp.zeros_like(state_ref)

    n_chunks = q_ref.shape[1] // BLOCK
    low = _low_half_mask((BLOCK, LANES))
    same_head = (lax.broadcasted_iota(jnp.int32, (LANES, LANES), 0) < HEAD_DIM) == _low_half_mask((LANES, LANES))

    def head_mean(t):
        lo_sum = jnp.sum(jnp.where(low, t, 0.0), axis=-1, keepdims=True)
        hi_sum = jnp.sum(jnp.where(low, 0.0, t), axis=-1, keepdims=True)
        return jnp.where(low, lo_sum, hi_sum) * (1.0 / HEAD_DIM)

    tiles = [(p, n) for p in range(RET_HEADS // 2) for n in range(n_chunks)]
    window = lambda ref, p, n: ref[0, n * BLOCK:(n + 1) * BLOCK, p * LANES:(p + 1) * LANES]
    q = {t: window(q_ref, *t) for t in tiles}
    k = {t: window(k_ref, *t) for t in tiles}
    vs = {t: _stack_heads(window(v_ref, *t)) for t in tiles}
    scores = {t: _dot_nt(q[t], _stack_heads(k[t])) for t in tiles}
    kv = {}
    for t in tiles:
        kd = (k[t].astype(jnp.float32) * kdec_ref[t[0]]).astype(jnp.bfloat16)
        kv[t] = lax.dot_general(kd, window(v_ref, *t), (((0,), (0,)), ((), ())),
                                preferred_element_type=jnp.float32)
    o_intra = {t: _dot((scores[t] * decay_ref[t[0]]).astype(jnp.bfloat16), vs[t]) for t in tiles}
    o = {}
    for p in range(RET_HEADS // 2):
        state = state_ref[p]
        for n in range(n_chunks):
            qd = (q[p, n].astype(jnp.float32) * qdec_ref[p]).astype(jnp.bfloat16)
            o[p, n] = o_intra[p, n] + _dot(qd, state.astype(jnp.bfloat16))
            state = sdec_ref[p] * state + jnp.where(same_head, kv[p, n], 0.0)
        state_ref[p] = state
    for t in tiles:
        cen = o[t] - head_mean(o[t])
        var = head_mean(cen * cen)
        normed = cen * lax.rsqrt(var + NORM_EPS) * gain_ref[:, t[0] * LANES:(t[0] + 1) * LANES]
        gate = window(gate_ref, *t)
        o_ref[0, t[1] * BLOCK:(t[1] + 1) * BLOCK, t[0] * LANES:(t[0] + 1) * LANES] = (
            gate * jax.nn.sigmoid(gate) * normed)


def _retention_tables():
    h = jnp.arange(RET_HEADS, dtype=jnp.float32)
    log_gamma = jnp.log1p(-(2.0 ** (-5.0 - h)))
    pos = jnp.arange(BLOCK, dtype=jnp.float32)
    rel = pos[:, None] - pos[None, :]
    decay = jnp.where(rel[None] >= 0,
                      jnp.exp(jnp.maximum(rel, 0.0)[None] * log_gamma[:, None, None]), 0.0)
    decay = jnp.swapaxes(decay.reshape(RET_HEADS // 2, 2, BLOCK, BLOCK), 1, 2)
    decay = decay.reshape(RET_HEADS // 2, BLOCK, 2 * BLOCK)
    per_lane = lambda t: jnp.repeat(t.reshape(-1, RET_HEADS // 2, 2), HEAD_DIM, axis=2)
    to_pairs = lambda t: jnp.moveaxis(per_lane(t).reshape(-1, RET_HEADS // 2, LANES), 1, 0)
    q_dec = to_pairs(jnp.exp((pos + 1.0)[:, None] * log_gamma[None, :]))
    k_dec = to_pairs(jnp.exp((BLOCK - 1 - pos)[:, None] * log_gamma[None, :]))
    chunk_decay = to_pairs(jnp.exp(BLOCK * log_gamma)[None, :])
    s_dec = jnp.broadcast_to(jnp.swapaxes(chunk_decay, 1, 2), (RET_HEADS // 2, LANES, LANES))
    return decay, q_dec, k_dec, s_dec


def _retention(proj, gate, gn_gain):
    b, s, _ = proj.shape
    ct = min(TOKEN_TILE, s)
    tables = _retention_tables()
    slab = lambda blk: pl.BlockSpec((1, ct, RET_W), lambda bi, i: (bi, i, blk * LANES // RET_W))
    table_specs = [pl.BlockSpec(t.shape, lambda bi, i: (0, 0, 0)) for t in tables]
    return pl.pallas_call(
        _ret_kernel,
        grid=(b, s // ct),
        in_specs=[slab(QC_BLK), slab(KC_BLK), slab(VC_BLK), slab(0),
                  pl.BlockSpec((1, RET_W), lambda bi, i: (0, 0))] + table_specs,
        out_specs=slab(0),
        out_shape=jax.ShapeDtypeStruct((b, s, RET_W), jnp.float32),
        scratch_shapes=[pltpu.VMEM((RET_HEADS // 2, LANES, LANES), jnp.float32)],
        compiler_params=pltpu.CompilerParams(
            dimension_semantics=("arbitrary", "arbitrary"), vmem_limit_bytes=VMEM_LIMIT),
        name="retention",
    )(proj, proj, proj, gate, gn_gain, *tables)


FF_CHUNK = 512


def _out_mlp_kernel(oa_ref, ob_ref, oc_ref, x_ref, ga_ref, gb_ref, gpost_ref, gpre_ref, gmlp_ref,
                    wout_ref, wup_ref, wdown_ref, o_ref):
    na = _rms(oa_ref[...], ga_ref[...]).astype(jnp.bfloat16)
    nb = _rms(ob_ref[...], gb_ref[...]).astype(jnp.bfloat16)
    nc = oc_ref[...].astype(jnp.bfloat16)
    y = (_dot(na, wout_ref[0:SWA_Q_W, :]) + _dot(nb, wout_ref[SWA_Q_W:SWA_Q_W + SB_W, :])
         + _dot(nc, wout_ref[SWA_Q_W + SB_W:, :]))
    x1 = x_ref[...] + _rms(y, gpost_ref[...])
    hn = _rms(x1, gpre_ref[...]).astype(jnp.bfloat16)
    acc = jnp.zeros_like(x1)
    for f0 in range(0, D_FF, FF_CHUNK):
        u = jnp.maximum(_dot(hn, wup_ref[:, f0:f0 + FF_CHUNK]), 0.0)
        acc = acc + _dot((u * u).astype(jnp.bfloat16), wdown_ref[f0:f0 + FF_CHUNK, :])
    o_ref[...] = x1 + _rms(acc, gmlp_ref[...])


def _out_mlp(oa, ob, oc, x2d, ga, gb, g_post, g_pre, g_mlp, w_out, w_up, w_down):
    n_tok = x2d.shape[0]
    tm = min(TOKEN_TILE, n_tok)
    row = lambda w: pl.BlockSpec((tm, w), lambda i: (i, 0))
    const = lambda shape: pl.BlockSpec(shape, lambda i: (0, 0))
    weight = lambda shape: pl.BlockSpec(shape, lambda i: (0, 0), pipeline_mode=pl.Buffered(1))
    return pl.pallas_call(
        _out_mlp_kernel,
        grid=(n_tok // tm,),
        in_specs=[
            row(SWA_Q_W), row(SB_W), row(RET_W), row(D_MODEL),
            const((1, SWA_Q_W)), const((1, SB_W)), const((1, D_MODEL)), const((1, D_MODEL)),
            const((1, D_MODEL)),
            weight((D_MODEL, D_MODEL)), weight((D_MODEL, D_FF)), weight((D_FF, D_MODEL)),
        ],
        out_specs=row(D_MODEL),
        out_shape=jax.ShapeDtypeStruct((n_tok, D_MODEL), jnp.float32),
        compiler_params=pltpu.CompilerParams(
            dimension_semantics=("arbitrary",), vmem_limit_bytes=VMEM_LIMIT),
        name="out_proj_mlp",
    )(oa, ob, oc, x2d, ga, gb, g_post, g_pre, g_mlp, w_out, w_up, w_down)


def _rope_table(positions):
    inv_freq = ROPE_THETA ** (-jnp.arange(0, HEAD_DIM, 2, dtype=jnp.float32) / HEAD_DIM)
    ang = positions.astype(jnp.float32)[:, None] * inv_freq[None, :]
    cos = jnp.tile(jnp.cos(ang), (1, 4))
    sin = jnp.tile(jnp.concatenate([-jnp.sin(ang), jnp.sin(ang)], axis=1), (1, 2))
    return jnp.stack([cos, sin, cos * QK_SCALE, sin * QK_SCALE])


def _permute_swa_heads(t, axis):
    shape = t.shape
    t = t.reshape(shape[:axis] + (SWA_Q_HEADS, HEAD_DIM) + shape[axis + 1:])
    t = jnp.take(t, jnp.asarray(SWA_HEAD_ORDER), axis=axis)
    return t.reshape(shape)


def _reorder_w_in(w_in):
    edges = np.cumsum([0, SWA_Q_W, SWA_KV_W, SWA_KV_W, SB_W, SB_W, SB_W, RET_W, RET_W, RET_W, RET_W])
    qa, ka, va, qb, kb, vb, qc, kc, vc, gc = [w_in[..., a:e] for a, e in zip(edges[:-1], edges[1:])]
    return jnp.concatenate([_permute_swa_heads(qa, 2), qc, kc, vc, ka, va, qb, kb, vb, gc], axis=-1)


def kernel(x, positions, w_in, w_out, sinks, branch_gain, w_up, w_down,
           norm_mix_pre, norm_mix_post, norm_mlp_pre, norm_mlp_post):
    b, s, d = x.shape
    depth = w_in.shape[0]
    rope_tab = _rope_table(positions)
    w_in = _reorder_w_in(w_in).astype(jnp.bfloat16)
    w_out = jnp.concatenate([_permute_swa_heads(w_out[:, :SWA_Q_W], 1), w_out[:, SWA_Q_W:]],
                            axis=1).astype(jnp.bfloat16)
    gain_a = _permute_swa_heads(branch_gain[:, :SWA_Q_W], 1)
    w_up = w_up.astype(jnp.bfloat16)
    w_down = w_down.astype(jnp.bfloat16)
    row = lambda t: t.reshape(1, -1)

    x2d = x.reshape(b * s, d)
    for l in range(depth):
        proj, gate = _in_proj(x2d, row(norm_mix_pre[l]), w_in[l], rope_tab, s)
        proj = proj.reshape(b, s, PROJ_W)
        out_a = _swa(sinks[l], proj)
        out_b = _sb(proj)
        out_c = _retention(proj, gate.reshape(b, s, RET_W), row(branch_gain[l, SWA_Q_W + SB_W:]))
        x2d = _out_mlp(out_a.reshape(b * s, SWA_Q_W), out_b.reshape(b * s, SB_W),
                       out_c.reshape(b * s, RET_W), x2d,
                       row(gain_a[l]), row(branch_gain[l, SWA_Q_W:SWA_Q_W + SB_W]),
                       row(norm_mix_post[l]), row(norm_mlp_pre[l]), row(norm_mlp_post[l]),
                       w_out[l], w_up[l], w_down[l])
    return x2d.reshape(b, s, d)
```

```python
import numpy as np
import jax
import jax.numpy as jnp
from jax import lax
from jax.experimental import pallas as pl
from jax.experimental.pallas import tpu as pltpu

D_MODEL = 1024
HEAD_DIM = 64
SWA_Q_HEADS = 6
SWA_KV_HEADS = 2
SB_HEADS = 4
RET_HEADS = 6
BLOCK = 128
ROPE_THETA = 10000.0
D_FF = 4 * D_MODEL
NORM_EPS = 1e-6
LANES = 128

SWA_Q_W = SWA_Q_HEADS * HEAD_DIM
SWA_KV_W = SWA_KV_HEADS * HEAD_DIM
SB_W = SB_HEADS * HEAD_DIM
RET_W = RET_HEADS * HEAD_DIM
IN_W = SWA_Q_W + 2 * SWA_KV_W + 3 * SB_W + 4 * RET_W
PROJ_W = IN_W - RET_W

QA_BLK = 0
QC_BLK, KC_BLK, VC_BLK = 3, 6, 9
KA_BLK, VA_BLK = 12, 13
QB_BLK, KB_BLK, VB_BLK = 14, 16, 18
GATE_BLK = 20
QK_SCALE = HEAD_DIM ** -0.5

SWA_HEAD_ORDER = (0, 3, 1, 4, 2, 5)

SB_DEAD_SUM = 104.0
SB_FUSED_BLOCKS = 3
SB_QUERY_TILE = 2 * BLOCK

TOKEN_TILE = 512
VMEM_LIMIT = 56 * 1024 * 1024


def _rms(xf, gain):
    return xf * lax.rsqrt(jnp.mean(xf * xf, axis=-1, keepdims=True) + NORM_EPS) * gain


def _low_half_mask(shape):
    return lax.broadcasted_iota(jnp.int32, shape, len(shape) - 1) < HEAD_DIM


def _stack_heads(pair):
    low = _low_half_mask(pair.shape)
    zero = jnp.zeros_like(pair)
    return jnp.concatenate([jnp.where(low, pair, zero), jnp.where(low, zero, pair)], axis=0)


def _dot_nt(a, b):
    return lax.dot_general(a, b, (((1,), (1,)), ((), ())), preferred_element_type=jnp.float32)


def _dot(a, b):
    return jnp.dot(a, b, preferred_element_type=jnp.float32)


def _in_blk_kind(blk):
    if QA_BLK <= blk < QC_BLK or KC_BLK <= blk < VC_BLK:
        return 1, 1.0
    if QC_BLK <= blk < KC_BLK or blk == KA_BLK:
        return 0, 1.0
    if QB_BLK <= blk < KB_BLK:
        return None, QK_SCALE
    return None, 1.0


def _in_proj_kernel(x_ref, g_ref, w_ref, rope_ref, proj_ref, gate_ref):
    hn = _rms(x_ref[...], g_ref[...]).astype(jnp.bfloat16)
    first_half = (lax.broadcasted_iota(jnp.int32, (x_ref.shape[0], LANES), 1) % HEAD_DIM) < HEAD_DIM // 2
    chunk = 2 * LANES
    for c0 in range(0, IN_W, chunk):
        width = min(chunk, IN_W - c0)
        p = _dot(hn, w_ref[:, c0:c0 + width])
        for s0 in range(0, width, LANES):
            blk = (c0 + s0) // LANES
            pb = p[:, s0:s0 + LANES]
            table, scale = _in_blk_kind(blk)
            if table is not None:
                cos = rope_ref[2 * table]
                sin = rope_ref[2 * table + 1]
                rot = jnp.where(first_half, pltpu.roll(pb, LANES - HEAD_DIM // 2, 1),
                                pltpu.roll(pb, HEAD_DIM // 2, 1))
                pb = pb * cos + rot * sin
            elif scale != 1.0:
                pb = pb * scale
            if blk >= GATE_BLK:
                g0 = (blk - GATE_BLK) * LANES
                gate_ref[:, g0:g0 + LANES] = pb
            else:
                proj_ref[:, blk * LANES:(blk + 1) * LANES] = pb.astype(jnp.bfloat16)


def _in_proj(x2d, gain, w_bf16, rope_tab, seq):
    n_tok = x2d.shape[0]
    tm = min(TOKEN_TILE, seq)
    seq_tiles = seq // tm
    const = lambda i: (0, 0)
    return pl.pallas_call(
        _in_proj_kernel,
        grid=(n_tok // tm,),
        in_specs=[
            pl.BlockSpec((tm, D_MODEL), lambda i: (i, 0)),
            pl.BlockSpec((1, D_MODEL), const),
            pl.BlockSpec((D_MODEL, IN_W), const, pipeline_mode=pl.Buffered(1)),
            pl.BlockSpec((4, tm, LANES), lambda i: (0, i % seq_tiles, 0)),
        ],
        out_specs=[
            pl.BlockSpec((tm, PROJ_W), lambda i: (i, 0)),
            pl.BlockSpec((tm, RET_W), lambda i: (i, 0)),
        ],
        out_shape=[
            jax.ShapeDtypeStruct((n_tok, PROJ_W), jnp.bfloat16),
            jax.ShapeDtypeStruct((n_tok, RET_W), jnp.float32),
        ],
        compiler_params=pltpu.CompilerParams(
            dimension_semantics=("arbitrary",), vmem_limit_bytes=VMEM_LIMIT),
        name="in_proj",
    )(x2d, gain, w_bf16, rope_tab)


def _swa_kernel(sink_ref, q_ref, k_ref, kp_ref, v_ref, vp_ref, o_ref):
    i = pl.program_id(1)
    n_blocks = q_ref.shape[1] // BLOCK
    rows = lax.broadcasted_iota(jnp.int32, (BLOCK, 2 * BLOCK), 0)
    cols = lax.broadcasted_iota(jnp.int32, (BLOCK, 2 * BLOCK), 1)
    own = (cols % BLOCK) <= rows
    low_lanes = _low_half_mask((BLOCK, LANES))
    first_bias = jnp.where(i > 0, 0.0, -jnp.inf).astype(jnp.float32)
    ks = [_stack_heads(kp_ref[0])] + [_stack_heads(k_ref[0, n * BLOCK:(n + 1) * BLOCK, :]) for n in range(n_blocks)]
    vs = [_stack_heads(vp_ref[0])] + [_stack_heads(v_ref[0, n * BLOCK:(n + 1) * BLOCK, :]) for n in range(n_blocks)]
    tiles = [(n, c) for n in range(n_blocks) for c in range(SWA_Q_HEADS // 2)]
    scores = {}
    for n, c in tiles:
        q = q_ref[0, n * BLOCK:(n + 1) * BLOCK, c * LANES:(c + 1) * LANES]
        s_prev = _dot_nt(q, ks[n])
        if n == 0:
            s_prev = s_prev + first_bias
        scores[n, c] = jnp.where(own, _dot_nt(q, ks[n + 1]), s_prev)
    weights = {}
    for n, c in tiles:
        probs, scales = [], []
        for half, head in enumerate(SWA_HEAD_ORDER[2 * c:2 * c + 2]):
            sh = scores[n, c][:, half * BLOCK:(half + 1) * BLOCK]
            sink = sink_ref[head]
            m = jnp.maximum(jnp.max(sh, axis=-1, keepdims=True), sink)
            p = jnp.exp(sh - m)
            probs.append(p)
            scales.append(1.0 / (jnp.sum(p, axis=-1, keepdims=True) + jnp.exp(sink - m)))
        p = jnp.concatenate(probs, axis=1)
        weights[n, c] = (jnp.where(own, p, 0.0).astype(jnp.bfloat16), jnp.where(own, 0.0, p).astype(jnp.bfloat16),
                         jnp.where(low_lanes, scales[0], scales[1]))
    for n, c in tiles:
        p_own, p_prev, scale = weights[n, c]
        o = _dot(p_own, vs[n + 1]) + _dot(p_prev, vs[n])
        o_ref[0, n * BLOCK:(n + 1) * BLOCK, c * LANES:(c + 1) * LANES] = o * scale


def _swa(sinks, proj):
    b, s, _ = proj.shape
    qt = min(TOKEN_TILE, s)
    per = qt // BLOCK
    prev = lambda i: jnp.maximum(i * per - 1, 0)
    return pl.pallas_call(
        _swa_kernel,
        grid=(b, s // qt),
        in_specs=[
            pl.BlockSpec(memory_space=pltpu.SMEM),
            pl.BlockSpec((1, qt, SWA_Q_W), lambda bi, i: (bi, i, QA_BLK)),
            pl.BlockSpec((1, qt, LANES), lambda bi, i: (bi, i, KA_BLK)),
            pl.BlockSpec((1, BLOCK, LANES), lambda bi, i: (bi, prev(i), KA_BLK)),
            pl.BlockSpec((1, qt, LANES), lambda bi, i: (bi, i, VA_BLK)),
            pl.BlockSpec((1, BLOCK, LANES), lambda bi, i: (bi, prev(i), VA_BLK)),
        ],
        out_specs=pl.BlockSpec((1, qt, SWA_Q_W), lambda bi, i: (bi, i, 0)),
        out_shape=jax.ShapeDtypeStruct((b, s, SWA_Q_W), jnp.float32),
        compiler_params=pltpu.CompilerParams(
            dimension_semantics=("arbitrary", "arbitrary"), vmem_limit_bytes=VMEM_LIMIT),
        name="swa",
    )(sinks, proj, proj, proj, proj, proj)


def _sb_kernel(q_ref, k_ref, v_ref, o_ref):
    i = pl.program_id(1)
    pairs = SB_HEADS // 2
    subs = q_ref.shape[1] // BLOCK
    wide = 2 * BLOCK
    units = [(u, p) for u in range(subs) for p in range(pairs)]
    q = {(u, p): q_ref[0, u * BLOCK:(u + 1) * BLOCK, p * LANES:(p + 1) * LANES] for u, p in units}
    rows = lax.broadcasted_iota(jnp.int32, (BLOCK, wide), 0)
    cols = lax.broadcasted_iota(jnp.int32, (BLOCK, wide), 1)
    strict = (cols % BLOCK) < rows
    kj = lax.broadcasted_iota(jnp.int32, (wide, wide), 0)
    ks = lax.broadcasted_iota(jnp.int32, (wide, wide), 1)
    tail_mat = jnp.where(((kj < BLOCK) == (ks < BLOCK)) & (kj >= ks), 1.0, 0.0).astype(jnp.bfloat16)

    def load(ref, p, j):
        start = pl.multiple_of(j * BLOCK, BLOCK)
        return _stack_heads(ref[0, pl.ds(start, BLOCK), p * LANES:(p + 1) * LANES])

    def both_heads(col_lo, col_hi):
        return jnp.concatenate([jnp.broadcast_to(col_lo, (BLOCK, BLOCK)),
                                jnp.broadcast_to(col_hi, (BLOCK, BLOCK))], axis=1)

    def tiles(jobs, carry, acc):
        keys, vals = {}, {}
        for u, p in units:
            for j, _ in jobs[u]:
                if (p, id(j)) not in keys:
                    keys[p, id(j)], vals[p, id(j)] = load(k_ref, p, j), load(v_ref, p, j)
        z_cat = {(u, p): _dot_nt(q[u, p], jnp.concatenate([keys[p, id(j)] for j, _ in jobs[u]], axis=0))
                 for u, p in units}
        zs, splits = {}, {}
        for u, p in units:
            for t, (_, mask) in enumerate(jobs[u]):
                z = z_cat[u, p][:, t * wide:(t + 1) * wide]
                if isinstance(mask, str):
                    z = jnp.where(strict, z, -jnp.inf)
                elif mask is not None:
                    z = z + mask
                sp = jnp.maximum(z, 0.0) + jnp.log(1.0 + jnp.exp(-jnp.abs(z)))
                hi = sp.astype(jnp.bfloat16)
                lo = (sp - hi.astype(jnp.float32)).astype(jnp.bfloat16)
                zs[u, p, t] = z
                splits[u, p, t] = (sp, hi, lo)
        incl = {key: _dot(hi, tail_mat) + _dot(lo, tail_mat) for key, (_, hi, lo) in splits.items()}
        new_carry, new_acc = {}, {}
        for u, p in units:
            c = carry[u, p]
            ws = []
            for t in range(len(jobs[u])):
                sp = splits[u, p, t][0]
                ws.append(jnp.exp(zs[u, p, t] - incl[u, p, t] - c).astype(jnp.bfloat16))
                c = c + both_heads(jnp.sum(sp[:, :BLOCK], axis=-1, keepdims=True),
                                   jnp.sum(sp[:, BLOCK:], axis=-1, keepdims=True))
            new_carry[u, p] = c
            v_cat = jnp.concatenate([vals[p, id(j)] for j, _ in jobs[u]], axis=0)
            new_acc[u, p] = acc[u, p] + _dot(jnp.concatenate(ws, axis=1), v_cat)
        return new_carry, new_acc

    def bias_if(valid):
        return jnp.where(valid, 0.0, -jnp.inf).astype(jnp.float32)

    first = i * subs
    carry = {key: jnp.zeros((BLOCK, wide), jnp.float32) for key in units}
    acc = {key: jnp.zeros((BLOCK, LANES), jnp.float32) for key in units}
    block_at = {d: jnp.maximum(first + d, 0) for d in range(1 - SB_FUSED_BLOCKS, subs)}
    fused = [[(block_at[u - back],
               "diag" if back == 0 else (None if u >= back else bias_if(first + u - back >= 0)))
              for back in range(SB_FUSED_BLOCKS)] for u in range(subs)]
    carry, acc = tiles(fused, carry, acc)

    def alive_after(cs):
        least = cs[units[0]]
        for key in units[1:]:
            least = jnp.minimum(least, cs[key])
        return (jnp.min(least) < SB_DEAD_SUM).astype(jnp.int32)

    flat = lambda d: tuple(d[key] for key in units)
    unflat = lambda t: dict(zip(units, t))

    def cond(state):
        back, alive, _, _ = state
        return jnp.logical_and(first + subs - 1 - back >= 0, alive > 0)

    def body(state):
        back, _, cs, accs = state
        jobs = [[(jnp.maximum(first + u - back, 0),
                  None if u == subs - 1 else bias_if(first + u - back >= 0))] for u in range(subs)]
        cs, accs = tiles(jobs, unflat(cs), unflat(accs))
        return back + 1, alive_after(cs), flat(cs), flat(accs)

    state = (jnp.int32(SB_FUSED_BLOCKS), alive_after(carry), flat(carry), flat(acc))
    acc = unflat(lax.while_loop(cond, body, state)[3])
    for u, p in units:
        o_ref[0, u * BLOCK:(u + 1) * BLOCK, p * LANES:(p + 1) * LANES] = acc[u, p]


def _sb(proj):
    b, s, _ = proj.shape
    qt = min(SB_QUERY_TILE, s)
    return pl.pallas_call(
        _sb_kernel,
        grid=(b, s // qt),
        in_specs=[
            pl.BlockSpec((1, qt, SB_W), lambda bi, i: (bi, i, QB_BLK * LANES // SB_W)),
            pl.BlockSpec((1, s, SB_W), lambda bi, i: (bi, 0, KB_BLK * LANES // SB_W)),
            pl.BlockSpec((1, s, SB_W), lambda bi, i: (bi, 0, VB_BLK * LANES // SB_W)),
        ],
        out_specs=pl.BlockSpec((1, qt, SB_W), lambda bi, i: (bi, i, 0)),
        out_shape=jax.ShapeDtypeStruct((b, s, SB_W), jnp.float32),
        compiler_params=pltpu.CompilerParams(
            dimension_semantics=("arbitrary", "arbitrary"), vmem_limit_bytes=VMEM_LIMIT),
        name="stick_breaking",
    )(proj, proj, proj)


def _ret_kernel(q_ref, k_ref, v_ref, gate_ref, gain_ref, decay_ref, qdec_ref, kdec_ref, sdec_ref,
                o_ref, state_ref):
    @pl.when(pl.program_id(1) == 0)
    def _():
        state_ref[...] = jnp.zeros_like(state_ref)

    n_chunks = q_ref.shape[1] // BLOCK
    low = _low_half_mask((BLOCK, LANES))
    same_head = (lax.broadcasted_iota(jnp.int32, (LANES, LANES), 0) < HEAD_DIM) == _low_half_mask((LANES, LANES))

    def head_mean(t):
        lo_sum = jnp.sum(jnp.where(low, t, 0.0), axis=-1, keepdims=True)
        hi_sum = jnp.sum(jnp.where(low, 0.0, t), axis=-1, keepdims=True)
        return jnp.where(low, lo_sum, hi_sum) * (1.0 / HEAD_DIM)

    tiles = [(p, n) for p in range(RET_HEADS // 2) for n in range(n_chunks)]
    window = lambda ref, p, n: ref[0, n * BLOCK:(n + 1) * BLOCK, p * LANES:(p + 1) * LANES]
    q = {t: window(q_ref, *t) for t in tiles}
    k = {t: window(k_ref, *t) for t in tiles}
    vs = {t: _stack_heads(window(v_ref, *t)) for t in tiles}
    scores = {t: _dot_nt(q[t], _stack_heads(k[t])) for t in tiles}
    kv = {}
    for t in tiles:
        kd = (k[t].astype(jnp.float32) * kdec_ref[t[0]]).astype(jnp.bfloat16)
        kv[t] = lax.dot_general(kd, window(v_ref, *t), (((0,), (0,)), ((), ())),
                                preferred_element_type=jnp.float32)
    o_intra = {t: _dot((scores[t] * decay_ref[t[0]]).astype(jnp.bfloat16), vs[t]) for t in tiles}
    o = {}
    for p in range(RET_HEADS // 2):
        state = state_ref[p]
        for n in range(n_chunks):
            qd = (q[p, n].astype(jnp.float32) * qdec_ref[p]).astype(jnp.bfloat16)
            o[p, n] = o_intra[p, n] + _dot(qd, state.astype(jnp.bfloat16))
            state = sdec_ref[p] * state + jnp.where(same_head, kv[p, n], 0.0)
        state_ref[p] = state
    for t in tiles:
        cen = o[t] - head_mean(o[t])
        var = head_mean(cen * cen)
        normed = cen * lax.rsqrt(var + NORM_EPS) * gain_ref[:, t[0] * LANES:(t[0] + 1) * LANES]
        gate = window(gate_ref, *t)
        o_ref[0, t[1] * BLOCK:(t[1] + 1) * BLOCK, t[0] * LANES:(t[0] + 1) * LANES] = (
            gate * jax.nn.sigmoid(gate) * normed)


def _retention_tables():
    h = jnp.arange(RET_HEADS, dtype=jnp.float32)
    log_gamma = jnp.log1p(-(2.0 ** (-5.0 - h)))
    pos = jnp.arange(BLOCK, dtype=jnp.float32)
    rel = pos[:, None] - pos[None, :]
    decay = jnp.where(rel[None] >= 0,
                      jnp.exp(jnp.maximum(rel, 0.0)[None] * log_gamma[:, None, None]), 0.0)
    decay = jnp.swapaxes(decay.reshape(RET_HEADS // 2, 2, BLOCK, BLOCK), 1, 2)
    decay = decay.reshape(RET_HEADS // 2, BLOCK, 2 * BLOCK)
    per_lane = lambda t: jnp.repeat(t.reshape(-1, RET_HEADS // 2, 2), HEAD_DIM, axis=2)
    to_pairs = lambda t: jnp.moveaxis(per_lane(t).reshape(-1, RET_HEADS // 2, LANES), 1, 0)
    q_dec = to_pairs(jnp.exp((pos + 1.0)[:, None] * log_gamma[None, :]))
    k_dec = to_pairs(jnp.exp((BLOCK - 1 - pos)[:, None] * log_gamma[None, :]))
    chunk_decay = to_pairs(jnp.exp(BLOCK * log_gamma)[None, :])
    s_dec = jnp.broadcast_to(jnp.swapaxes(chunk_decay, 1, 2), (RET_HEADS // 2, LANES, LANES))
    return decay, q_dec, k_dec, s_dec


def _retention(proj, gate, gn_gain):
    b, s, _ = proj.shape
    ct = min(TOKEN_TILE, s)
    tables = _retention_tables()
    slab = lambda blk: pl.BlockSpec((1, ct, RET_W), lambda bi, i: (bi, i, blk * LANES // RET_W))
    table_specs = [pl.BlockSpec(t.shape, lambda bi, i: (0, 0, 0)) for t in tables]
    return pl.pallas_call(
        _ret_kernel,
        grid=(b, s // ct),
        in_specs=[slab(QC_BLK), slab(KC_BLK), slab(VC_BLK), slab(0),
                  pl.BlockSpec((1, RET_W), lambda bi, i: (0, 0))] + table_specs,
        out_specs=slab(0),
        out_shape=jax.ShapeDtypeStruct((b, s, RET_W), jnp.float32),
        scratch_shapes=[pltpu.VMEM((RET_HEADS // 2, LANES, LANES), jnp.float32)],
        compiler_params=pltpu.CompilerParams(
            dimension_semantics=("arbitrary", "arbitrary"), vmem_limit_bytes=VMEM_LIMIT),
        name="retention",
    )(proj, proj, proj, gate, gn_gain, *tables)


FF_CHUNK = 512


def _out_mlp_kernel(oa_ref, ob_ref, oc_ref, x_ref, ga_ref, gb_ref, gpost_ref, gpre_ref, gmlp_ref,
                    wout_ref, wup_ref, wdown_ref, o_ref):
    na = _rms(oa_ref[...], ga_ref[...]).astype(jnp.bfloat16)
    nb = _rms(ob_ref[...], gb_ref[...]).astype(jnp.bfloat16)
    nc = oc_ref[...].astype(jnp.bfloat16)
    y = (_dot(na, wout_ref[0:SWA_Q_W, :]) + _dot(nb, wout_ref[SWA_Q_W:SWA_Q_W + SB_W, :])
         + _dot(nc, wout_ref[SWA_Q_W + SB_W:, :]))
    x1 = x_ref[...] + _rms(y, gpost_ref[...])
    hn = _rms(x1, gpre_ref[...]).astype(jnp.bfloat16)
    acc = jnp.zeros_like(x1)
    for f0 in range(0, D_FF, FF_CHUNK):
        u = jnp.maximum(_dot(hn, wup_ref[:, f0:f0 + FF_CHUNK]), 0.0)
        acc = acc + _dot((u * u).astype(jnp.bfloat16), wdown_ref[f0:f0 + FF_CHUNK, :])
    o_ref[...] = x1 + _rms(acc, gmlp_ref[...])


def _out_mlp(oa, ob, oc, x2d, ga, gb, g_post, g_pre, g_mlp, w_out, w_up, w_down):
    n_tok = x2d.shape[0]
    tm = min(TOKEN_TILE, n_tok)
    row = lambda w: pl.BlockSpec((tm, w), lambda i: (i, 0))
    const = lambda shape: pl.BlockSpec(shape, lambda i: (0, 0))
    weight = lambda shape: pl.BlockSpec(shape, lambda i: (0, 0), pipeline_mode=pl.Buffered(1))
    return pl.pallas_call(
        _out_mlp_kernel,
        grid=(n_tok // tm,),
        in_specs=[
            row(SWA_Q_W), row(SB_W), row(RET_W), row(D_MODEL),
            const((1, SWA_Q_W)), const((1, SB_W)), const((1, D_MODEL)), const((1, D_MODEL)),
            const((1, D_MODEL)),
            weight((D_MODEL, D_MODEL)), weight((D_MODEL, D_FF)), weight((D_FF, D_MODEL)),
        ],
        out_specs=row(D_MODEL),
        out_shape=jax.ShapeDtypeStruct((n_tok, D_MODEL), jnp.float32),
        compiler_params=pltpu.CompilerParams(
            dimension_semantics=("arbitrary",), vmem_limit_bytes=VMEM_LIMIT),
        name="out_proj_mlp",
    )(oa, ob, oc, x2d, ga, gb, g_post, g_pre, g_mlp, w_out, w_up, w_down)


def _rope_table(positions):
    inv_freq = ROPE_THETA ** (-jnp.arange(0, HEAD_DIM, 2, dtype=jnp.float32) / HEAD_DIM)
    ang = positions.astype(jnp.float32)[:, None] * inv_freq[None, :]
    cos = jnp.tile(jnp.cos(ang), (1, 4))
    sin = jnp.tile(jnp.concatenate([-jnp.sin(ang), jnp.sin(ang)], axis=1), (1, 2))
    return jnp.stack([cos, sin, cos * QK_SCALE, sin * QK_SCALE])


def _permute_swa_heads(t, axis):
    shape = t.shape
    t = t.reshape(shape[:axis] + (SWA_Q_HEADS, HEAD_DIM) + shape[axis + 1:])
    t = jnp.take(t, jnp.asarray(SWA_HEAD_ORDER), axis=axis)
    return t.reshape(shape)


def _reorder_w_in(w_in):
    edges = np.cumsum([0, SWA_Q_W, SWA_KV_W, SWA_KV_W, SB_W, SB_W, SB_W, RET_W, RET_W, RET_W, RET_W])
    qa, ka, va, qb, kb, vb, qc, kc, vc, gc = [w_in[..., a:e] for a, e in zip(edges[:-1], edges[1:])]
    return jnp.concatenate([_permute_swa_heads(qa, 2), qc, kc, vc, ka, va, qb, kb, vb, gc], axis=-1)


def kernel(x, positions, w_in, w_out, sinks, branch_gain, w_up, w_down,
           norm_mix_pre, norm_mix_post, norm_mlp_pre, norm_mlp_post):
    b, s, d = x.shape
    depth = w_in.shape[0]
    rope_tab = _rope_table(positions)
    w_in = _reorder_w_in(w_in).astype(jnp.bfloat16)
    w_out = jnp.concatenate([_permute_swa_heads(w_out[:, :SWA_Q_W], 1), w_out[:, SWA_Q_W:]],
                            axis=1).astype(jnp.bfloat16)
    gain_a = _permute_swa_heads(branch_gain[:, :SWA_Q_W], 1)
    w_up = w_up.astype(jnp.bfloat16)
    w_down = w_down.astype(jnp.bfloat16)
    row = lambda t: t.reshape(1, -1)

    x2d = x.reshape(b * s, d)
    for l in range(depth):
        proj, gate = _in_proj(x2d, row(norm_mix_pre[l]), w_in[l], rope_tab, s)
        proj = proj.reshape(b, s, PROJ_W)
        out_a = _swa(sinks[l], proj)
        out_b = _sb(proj)
        out_c = _retention(proj, gate.reshape(b, s, RET_W), row(branch_gain[l, SWA_Q_W + SB_W:]))
        x2d = _out_mlp(out_a.reshape(b * s, SWA_Q_W), out_b.reshape(b * s, SB_W),
                       out_c.reshape(b * s, RET_W), x2d,
                       row(gain_a[l]), row(branch_gain[l, SWA_Q_W:SWA_Q_W + SB_W]),
                       row(norm_mix_post[l]), row(norm_mlp_pre[l]), row(norm_mlp_post[l]),
                       w_out[l], w_up[l], w_down[l])
    return x2d.reshape(b, s, d)
```

```python
import itertools

import numpy as np
import jax
import jax.numpy as jnp
from jax import lax
from jax.experimental import pallas as pl
from jax.experimental.pallas import tpu as pltpu

D_MODEL = 1024
HEAD_DIM = 64
SWA_Q_HEADS = 6
SWA_KV_HEADS = 2
SB_HEADS = 4
RET_HEADS = 6
BLOCK = 128
ROPE_THETA = 10000.0
D_FF = 4 * D_MODEL
NORM_EPS = 1e-6
LANES = 128

SWA_Q_W = SWA_Q_HEADS * HEAD_DIM
SWA_KV_W = SWA_KV_HEADS * HEAD_DIM
SB_W = SB_HEADS * HEAD_DIM
RET_W = RET_HEADS * HEAD_DIM
IN_W = SWA_Q_W + 2 * SWA_KV_W + 3 * SB_W + 4 * RET_W
PROJ_W = IN_W - RET_W

QA_BLK = 0
QC_BLK, KC_BLK, VC_BLK = 3, 6, 9
KA_BLK, VA_BLK = 12, 13
QB_BLK, KB_BLK, VB_BLK = 14, 16, 18
GATE_BLK = 20
QK_SCALE = HEAD_DIM ** -0.5

SWA_HEAD_ORDER = (0, 3, 1, 4, 2, 5)

SB_DEAD_SUM = 104.0
SB_FUSED_BLOCKS = 3

TOKEN_TILE = 512
VMEM_LIMIT = 56 * 1024 * 1024


def _rms(xf, gain):
    return xf * lax.rsqrt(jnp.mean(xf * xf, axis=-1, keepdims=True) + NORM_EPS) * gain


def _low_half_mask(shape):
    return lax.broadcasted_iota(jnp.int32, shape, len(shape) - 1) < HEAD_DIM


def _stack_heads(pair):
    low = _low_half_mask(pair.shape)
    zero = jnp.zeros_like(pair)
    return jnp.concatenate([jnp.where(low, pair, zero), jnp.where(low, zero, pair)], axis=0)


def _dot_nt(a, b):
    return lax.dot_general(a, b, (((1,), (1,)), ((), ())), preferred_element_type=jnp.float32)


def _dot(a, b):
    return jnp.dot(a, b, preferred_element_type=jnp.float32)


def _in_blk_kind(blk):
    if QA_BLK <= blk < QC_BLK or KC_BLK <= blk < VC_BLK:
        return 1, 1.0
    if QC_BLK <= blk < KC_BLK or blk == KA_BLK:
        return 0, 1.0
    if QB_BLK <= blk < KB_BLK:
        return None, QK_SCALE
    return None, 1.0


def _in_proj_kernel(x_ref, g_ref, w_ref, rope_ref, proj_ref, gate_ref):
    hn = _rms(x_ref[...], g_ref[...]).astype(jnp.bfloat16)
    first_half = (lax.broadcasted_iota(jnp.int32, (x_ref.shape[0], LANES), 1) % HEAD_DIM) < HEAD_DIM // 2
    chunk = 2 * LANES
    for c0 in range(0, IN_W, chunk):
        width = min(chunk, IN_W - c0)
        p = _dot(hn, w_ref[:, c0:c0 + width])
        for s0 in range(0, width, LANES):
            blk = (c0 + s0) // LANES
            pb = p[:, s0:s0 + LANES]
            table, scale = _in_blk_kind(blk)
            if table is not None:
                cos = rope_ref[2 * table]
                sin = rope_ref[2 * table + 1]
                rot = jnp.where(first_half, pltpu.roll(pb, LANES - HEAD_DIM // 2, 1),
                                pltpu.roll(pb, HEAD_DIM // 2, 1))
                pb = pb * cos + rot * sin
            elif scale != 1.0:
                pb = pb * scale
            if blk >= GATE_BLK:
                g0 = (blk - GATE_BLK) * LANES
                gate_ref[:, g0:g0 + LANES] = pb
            else:
                proj_ref[:, blk * LANES:(blk + 1) * LANES] = pb.astype(jnp.bfloat16)


def _in_proj(x2d, gain, w_bf16, rope_tab, seq):
    n_tok = x2d.shape[0]
    tm = min(TOKEN_TILE, seq)
    seq_tiles = seq // tm
    const = lambda i: (0, 0)
    return pl.pallas_call(
        _in_proj_kernel,
        grid=(n_tok // tm,),
        in_specs=[
            pl.BlockSpec((tm, D_MODEL), lambda i: (i, 0)),
            pl.BlockSpec((1, D_MODEL), const),
            pl.BlockSpec((D_MODEL, IN_W), const, pipeline_mode=pl.Buffered(1)),
            pl.BlockSpec((4, tm, LANES), lambda i: (0, i % seq_tiles, 0)),
        ],
        out_specs=[
            pl.BlockSpec((tm, PROJ_W), lambda i: (i, 0)),
            pl.BlockSpec((tm, RET_W), lambda i: (i, 0)),
        ],
        out_shape=[
            jax.ShapeDtypeStruct((n_tok, PROJ_W), jnp.bfloat16),
            jax.ShapeDtypeStruct((n_tok, RET_W), jnp.float32),
        ],
        compiler_params=pltpu.CompilerParams(
            dimension_semantics=("arbitrary",), vmem_limit_bytes=VMEM_LIMIT),
        name="in_proj",
    )(x2d, gain, w_bf16, rope_tab)


def _swa_stages(i, sink_ref, q_ref, k_ref, kp_ref, v_ref, vp_ref, o_ref):
    n_blocks = q_ref.shape[1] // BLOCK
    rows = lax.broadcasted_iota(jnp.int32, (BLOCK, 2 * BLOCK), 0)
    cols = lax.broadcasted_iota(jnp.int32, (BLOCK, 2 * BLOCK), 1)
    own = (cols % BLOCK) <= rows
    low_lanes = _low_half_mask((BLOCK, LANES))
    first_bias = jnp.where(i > 0, 0.0, -jnp.inf).astype(jnp.float32)
    ks = [_stack_heads(kp_ref[0])] + [_stack_heads(k_ref[0, n * BLOCK:(n + 1) * BLOCK, :]) for n in range(n_blocks)]
    vs = [_stack_heads(vp_ref[0])] + [_stack_heads(v_ref[0, n * BLOCK:(n + 1) * BLOCK, :]) for n in range(n_blocks)]
    tiles = [(n, c) for n in range(n_blocks) for c in range(SWA_Q_HEADS // 2)]
    scores = {}
    for n, c in tiles:
        q = q_ref[0, n * BLOCK:(n + 1) * BLOCK, c * LANES:(c + 1) * LANES]
        s_prev = _dot_nt(q, ks[n])
        if n == 0:
            s_prev = s_prev + first_bias
        scores[n, c] = jnp.where(own, _dot_nt(q, ks[n + 1]), s_prev)
    yield
    weights = {}
    for n, c in tiles:
        probs, scales = [], []
        for half, head in enumerate(SWA_HEAD_ORDER[2 * c:2 * c + 2]):
            sh = scores[n, c][:, half * BLOCK:(half + 1) * BLOCK]
            sink = sink_ref[head]
            m = jnp.maximum(jnp.max(sh, axis=-1, keepdims=True), sink)
            p = jnp.exp(sh - m)
            probs.append(p)
            scales.append(1.0 / (jnp.sum(p, axis=-1, keepdims=True) + jnp.exp(sink - m)))
        p = jnp.concatenate(probs, axis=1)
        weights[n, c] = (jnp.where(own, p, 0.0).astype(jnp.bfloat16), jnp.where(own, 0.0, p).astype(jnp.bfloat16),
                         jnp.where(low_lanes, scales[0], scales[1]))
    for n, c in tiles:
        p_own, p_prev, scale = weights[n, c]
        o = _dot(p_own, vs[n + 1]) + _dot(p_prev, vs[n])
        o_ref[0, n * BLOCK:(n + 1) * BLOCK, c * LANES:(c + 1) * LANES] = o * scale
    yield


class _StickBreaking:
    def __init__(self, i, q_ref, k_ref, v_ref, o_ref):
        self.k_ref, self.v_ref, self.o_ref = k_ref, v_ref, o_ref
        self.subs = q_ref.shape[1] // BLOCK
        self.first = i * self.subs
        pairs = SB_HEADS // 2
        wide = 2 * BLOCK
        self.units = [(u, p) for u in range(self.subs) for p in range(pairs)]
        self.q = {(u, p): q_ref[0, u * BLOCK:(u + 1) * BLOCK, p * LANES:(p + 1) * LANES] for u, p in self.units}
        rows = lax.broadcasted_iota(jnp.int32, (BLOCK, wide), 0)
        cols = lax.broadcasted_iota(jnp.int32, (BLOCK, wide), 1)
        self.strict = (cols % BLOCK) < rows
        kj = lax.broadcasted_iota(jnp.int32, (wide, wide), 0)
        ks = lax.broadcasted_iota(jnp.int32, (wide, wide), 1)
        self.tail_mat = jnp.where(((kj < BLOCK) == (ks < BLOCK)) & (kj >= ks), 1.0, 0.0).astype(jnp.bfloat16)
        self.carry = {key: jnp.zeros((BLOCK, wide), jnp.float32) for key in self.units}
        self.acc = {key: jnp.zeros((BLOCK, LANES), jnp.float32) for key in self.units}

    def _load(self, ref, p, j):
        start = pl.multiple_of(j * BLOCK, BLOCK)
        return _stack_heads(ref[0, pl.ds(start, BLOCK), p * LANES:(p + 1) * LANES])

    @staticmethod
    def _both_heads(col_lo, col_hi):
        return jnp.concatenate([jnp.broadcast_to(col_lo, (BLOCK, BLOCK)),
                                jnp.broadcast_to(col_hi, (BLOCK, BLOCK))], axis=1)

    @staticmethod
    def _bias_if(valid):
        return jnp.where(valid, 0.0, -jnp.inf).astype(jnp.float32)

    def _tiles(self, jobs, carry, acc, out):
        wide = 2 * BLOCK
        keys, vals = {}, {}
        for u, p in self.units:
            for j, _ in jobs[u]:
                if (p, id(j)) not in keys:
                    keys[p, id(j)], vals[p, id(j)] = self._load(self.k_ref, p, j), self._load(self.v_ref, p, j)
        z_cat = {(u, p): _dot_nt(self.q[u, p], jnp.concatenate([keys[p, id(j)] for j, _ in jobs[u]], axis=0))
                 for u, p in self.units}
        yield
        zs, splits = {}, {}
        for u, p in self.units:
            for t, (_, mask) in enumerate(jobs[u]):
                z = z_cat[u, p][:, t * wide:(t + 1) * wide]
                if isinstance(mask, str):
                    z = jnp.where(self.strict, z, -jnp.inf)
                elif mask is not None:
                    z = z + mask
                sp = jnp.maximum(z, 0.0) + jnp.log(1.0 + jnp.exp(-jnp.abs(z)))
                hi = sp.astype(jnp.bfloat16)
                lo = (sp - hi.astype(jnp.float32)).astype(jnp.bfloat16)
                zs[u, p, t] = z
                splits[u, p, t] = (sp, hi, lo)
        incl = {key: _dot(hi, self.tail_mat) + _dot(lo, self.tail_mat) for key, (_, hi, lo) in splits.items()}
        yield
        new_carry, new_acc = {}, {}
        for u, p in self.units:
            c = carry[u, p]
            ws = []
            for t in range(len(jobs[u])):
                sp = splits[u, p, t][0]
                ws.append(jnp.exp(zs[u, p, t] - incl[u, p, t] - c).astype(jnp.bfloat16))
                c = c + self._both_heads(jnp.sum(sp[:, :BLOCK], axis=-1, keepdims=True),
                                         jnp.sum(sp[:, BLOCK:], axis=-1, keepdims=True))
            new_carry[u, p] = c
            v_cat = jnp.concatenate([vals[p, id(j)] for j, _ in jobs[u]], axis=0)
            new_acc[u, p] = acc[u, p] + _dot(jnp.concatenate(ws, axis=1), v_cat)
        out["carry"], out["acc"] = new_carry, new_acc
        yield

    def fused_stages(self):
        first, subs = self.first, self.subs
        block_at = {d: jnp.maximum(first + d, 0) for d in range(1 - SB_FUSED_BLOCKS, subs)}
        jobs = [[(block_at[u - back],
                  "diag" if back == 0 else (None if u >= back else self._bias_if(first + u - back >= 0)))
                 for back in range(SB_FUSED_BLOCKS)] for u in range(subs)]
        out = {}
        yield from self._tiles(jobs, self.carry, self.acc, out)
        self.carry, self.acc = out["carry"], out["acc"]

    def finish(self):
        first, subs, units = self.first, self.subs, self.units

        def alive_after(cs):
            least = cs[units[0]]
            for key in units[1:]:
                least = jnp.minimum(least, cs[key])
            return (jnp.min(least) < SB_DEAD_SUM).astype(jnp.int32)

        flat = lambda d: tuple(d[key] for key in units)
        unflat = lambda t: dict(zip(units, t))

        def cond(state):
            back, alive, _, _ = state
            return jnp.logical_and(first + subs - 1 - back >= 0, alive > 0)

        def body(state):
            back, _, cs, accs = state
            jobs = [[(jnp.maximum(first + u - back, 0),
                      None if u == subs - 1 else self._bias_if(first + u - back >= 0))] for u in range(subs)]
            out = {}
            for _ in self._tiles(jobs, unflat(cs), unflat(accs), out):
                pass
            return back + 1, alive_after(out["carry"]), flat(out["carry"]), flat(out["acc"])

        state = (jnp.int32(SB_FUSED_BLOCKS), alive_after(self.carry), flat(self.carry), flat(self.acc))
        acc = unflat(lax.while_loop(cond, body, state)[3])
        for u, p in units:
            self.o_ref[0, u * BLOCK:(u + 1) * BLOCK, p * LANES:(p + 1) * LANES] = acc[u, p]


def _ret_stages(q_ref, k_ref, v_ref, gate_ref, gain_ref, decay_ref, qdec_ref, kdec_ref, sdec_ref,
                o_ref, state_ref):
    n_chunks = q_ref.shape[1] // BLOCK
    low = _low_half_mask((BLOCK, LANES))
    same_head = (lax.broadcasted_iota(jnp.int32, (LANES, LANES), 0) < HEAD_DIM) == _low_half_mask((LANES, LANES))

    def head_mean(t):
        lo_sum = jnp.sum(jnp.where(low, t, 0.0), axis=-1, keepdims=True)
        hi_sum = jnp.sum(jnp.where(low, 0.0, t), axis=-1, keepdims=True)
        return jnp.where(low, lo_sum, hi_sum) * (1.0 / HEAD_DIM)

    tiles = [(p, n) for p in range(RET_HEADS // 2) for n in range(n_chunks)]
    window = lambda ref, p, n: ref[0, n * BLOCK:(n + 1) * BLOCK, p * LANES:(p + 1) * LANES]
    q = {t: window(q_ref, *t) for t in tiles}
    k = {t: window(k_ref, *t) for t in tiles}
    vs = {t: _stack_heads(window(v_ref, *t)) for t in tiles}
    scores = {t: _dot_nt(q[t], _stack_heads(k[t])) for t in tiles}
    kv = {}
    for t in tiles:
        kd = (k[t].astype(jnp.float32) * kdec_ref[t[0]]).astype(jnp.bfloat16)
        kv[t] = lax.dot_general(kd, window(v_ref, *t), (((0,), (0,)), ((), ())),
                                preferred_element_type=jnp.float32)
    yield
    o_intra = {t: _dot((scores[t] * decay_ref[t[0]]).astype(jnp.bfloat16), vs[t]) for t in tiles}
    yield
    o = {}
    for p in range(RET_HEADS // 2):
        state = state_ref[p]
        for n in range(n_chunks):
            qd = (q[p, n].astype(jnp.float32) * qdec_ref[p]).astype(jnp.bfloat16)
            o[p, n] = o_intra[p, n] + _dot(qd, state.astype(jnp.bfloat16))
            state = sdec_ref[p] * state + jnp.where(same_head, kv[p, n], 0.0)
        state_ref[p] = state
    for t in tiles:
        cen = o[t] - head_mean(o[t])
        var = head_mean(cen * cen)
        normed = cen * lax.rsqrt(var + NORM_EPS) * gain_ref[:, t[0] * LANES:(t[0] + 1) * LANES]
        gate = window(gate_ref, *t)
        o_ref[0, t[1] * BLOCK:(t[1] + 1) * BLOCK, t[0] * LANES:(t[0] + 1) * LANES] = (
            gate * jax.nn.sigmoid(gate) * normed)
    yield


def _mixer_kernel(sink_ref, qa_ref, ka_ref, kap_ref, va_ref, vap_ref, qb_ref, kb_ref, vb_ref,
                  qc_ref, kc_ref, vc_ref, gate_ref, gain_ref, decay_ref, qdec_ref, kdec_ref, sdec_ref,
                  oa_ref, ob_ref, oc_ref, state_ref):
    i = pl.program_id(1)

    @pl.when(i == 0)
    def _():
        state_ref[...] = jnp.zeros_like(state_ref)

    sb = _StickBreaking(i, qb_ref, kb_ref, vb_ref, ob_ref)
    stages = [sb.fused_stages(),
              _swa_stages(i, sink_ref, qa_ref, ka_ref, kap_ref, va_ref, vap_ref, oa_ref),
              _ret_stages(qc_ref, kc_ref, vc_ref, gate_ref, gain_ref, decay_ref, qdec_ref, kdec_ref, sdec_ref,
                          oc_ref, state_ref)]
    for _ in itertools.zip_longest(*stages):
        pass
    sb.finish()


def _retention_tables():
    h = jnp.arange(RET_HEADS, dtype=jnp.float32)
    log_gamma = jnp.log1p(-(2.0 ** (-5.0 - h)))
    pos = jnp.arange(BLOCK, dtype=jnp.float32)
    rel = pos[:, None] - pos[None, :]
    decay = jnp.where(rel[None] >= 0,
                      jnp.exp(jnp.maximum(rel, 0.0)[None] * log_gamma[:, None, None]), 0.0)
    decay = jnp.swapaxes(decay.reshape(RET_HEADS // 2, 2, BLOCK, BLOCK), 1, 2)
    decay = decay.reshape(RET_HEADS // 2, BLOCK, 2 * BLOCK)
    per_lane = lambda t: jnp.repeat(t.reshape(-1, RET_HEADS // 2, 2), HEAD_DIM, axis=2)
    to_pairs = lambda t: jnp.moveaxis(per_lane(t).reshape(-1, RET_HEADS // 2, LANES), 1, 0)
    q_dec = to_pairs(jnp.exp((pos + 1.0)[:, None] * log_gamma[None, :]))
    k_dec = to_pairs(jnp.exp((BLOCK - 1 - pos)[:, None] * log_gamma[None, :]))
    chunk_decay = to_pairs(jnp.exp(BLOCK * log_gamma)[None, :])
    s_dec = jnp.broadcast_to(jnp.swapaxes(chunk_decay, 1, 2), (RET_HEADS // 2, LANES, LANES))
    return decay, q_dec, k_dec, s_dec


def _mixers(sinks, proj, gate, gn_gain):
    b, s, _ = proj.shape
    tt = min(TOKEN_TILE, s)
    per = tt // BLOCK
    prev = lambda i: jnp.maximum(i * per - 1, 0)
    tables = _retention_tables()
    tile = lambda width, blk: pl.BlockSpec((1, tt, width), lambda bi, i: (bi, i, blk * LANES // width))
    whole = lambda width, blk: pl.BlockSpec((1, s, width), lambda bi, i: (bi, 0, blk * LANES // width))
    out_tile = lambda width: pl.BlockSpec((1, tt, width), lambda bi, i: (bi, i, 0))
    return pl.pallas_call(
        _mixer_kernel,
        grid=(b, s // tt),
        in_specs=[
            pl.BlockSpec(memory_space=pltpu.SMEM),
            tile(SWA_Q_W, QA_BLK), tile(LANES, KA_BLK),
            pl.BlockSpec((1, BLOCK, LANES), lambda bi, i: (bi, prev(i), KA_BLK)),
            tile(LANES, VA_BLK),
            pl.BlockSpec((1, BLOCK, LANES), lambda bi, i: (bi, prev(i), VA_BLK)),
            tile(SB_W, QB_BLK), whole(SB_W, KB_BLK), whole(SB_W, VB_BLK),
            tile(RET_W, QC_BLK), tile(RET_W, KC_BLK), tile(RET_W, VC_BLK), tile(RET_W, 0),
            pl.BlockSpec((1, RET_W), lambda bi, i: (0, 0)),
        ] + [pl.BlockSpec(t.shape, lambda bi, i: (0, 0, 0)) for t in tables],
        out_specs=[out_tile(SWA_Q_W), out_tile(SB_W), out_tile(RET_W)],
        out_shape=[jax.ShapeDtypeStruct((b, s, SWA_Q_W), jnp.float32),
                   jax.ShapeDtypeStruct((b, s, SB_W), jnp.float32),
                   jax.ShapeDtypeStruct((b, s, RET_W), jnp.float32)],
        scratch_shapes=[pltpu.VMEM((RET_HEADS // 2, LANES, LANES), jnp.float32)],
        compiler_params=pltpu.CompilerParams(
            dimension_semantics=("arbitrary", "arbitrary"), vmem_limit_bytes=VMEM_LIMIT),
        name="mixers",
    )(sinks, proj, proj, proj, proj, proj, proj, proj, proj, proj, proj, proj, gate, gn_gain, *tables)


FF_CHUNK = 512


def _out_mlp_kernel(oa_ref, ob_ref, oc_ref, x_ref, ga_ref, gb_ref, gpost_ref, gpre_ref, gmlp_ref,
                    wout_ref, wup_ref, wdown_ref, o_ref):
    na = _rms(oa_ref[...], ga_ref[...]).astype(jnp.bfloat16)
    nb = _rms(ob_ref[...], gb_ref[...]).astype(jnp.bfloat16)
    nc = oc_ref[...].astype(jnp.bfloat16)
    y = (_dot(na, wout_ref[0:SWA_Q_W, :]) + _dot(nb, wout_ref[SWA_Q_W:SWA_Q_W + SB_W, :])
         + _dot(nc, wout_ref[SWA_Q_W + SB_W:, :]))
    x1 = x_ref[...] + _rms(y, gpost_ref[...])
    hn = _rms(x1, gpre_ref[...]).astype(jnp.bfloat16)
    acc = jnp.zeros_like(x1)
    for f0 in range(0, D_FF, FF_CHUNK):
        u = jnp.maximum(_dot(hn, wup_ref[:, f0:f0 + FF_CHUNK]), 0.0)
        acc = acc + _dot((u * u).astype(jnp.bfloat16), wdown_ref[f0:f0 + FF_CHUNK, :])
    o_ref[...] = x1 + _rms(acc, gmlp_ref[...])


def _out_mlp(oa, ob, oc, x2d, ga, gb, g_post, g_pre, g_mlp, w_out, w_up, w_down):
    n_tok = x2d.shape[0]
    tm = min(TOKEN_TILE, n_tok)
    row = lambda w: pl.BlockSpec((tm, w), lambda i: (i, 0))
    const = lambda shape: pl.BlockSpec(shape, lambda i: (0, 0))
    weight = lambda shape: pl.BlockSpec(shape, lambda i: (0, 0), pipeline_mode=pl.Buffered(1))
    return pl.pallas_call(
        _out_mlp_kernel,
        grid=(n_tok // tm,),
        in_specs=[
            row(SWA_Q_W), row(SB_W), row(RET_W), row(D_MODEL),
            const((1, SWA_Q_W)), const((1, SB_W)), const((1, D_MODEL)), const((1, D_MODEL)),
            const((1, D_MODEL)),
            weight((D_MODEL, D_MODEL)), weight((D_MODEL, D_FF)), weight((D_FF, D_MODEL)),
        ],
        out_specs=row(D_MODEL),
        out_shape=jax.ShapeDtypeStruct((n_tok, D_MODEL), jnp.float32),
        compiler_params=pltpu.CompilerParams(
            dimension_semantics=("arbitrary",), vmem_limit_bytes=VMEM_LIMIT),
        name="out_proj_mlp",
    )(oa, ob, oc, x2d, ga, gb, g_post, g_pre, g_mlp, w_out, w_up, w_down)


def _rope_table(positions):
    inv_freq = ROPE_THETA ** (-jnp.arange(0, HEAD_DIM, 2, dtype=jnp.float32) / HEAD_DIM)
    ang = positions.astype(jnp.float32)[:, None] * inv_freq[None, :]
    cos = jnp.tile(jnp.cos(ang), (1, 4))
    sin = jnp.tile(jnp.concatenate([-jnp.sin(ang), jnp.sin(ang)], axis=1), (1, 2))
    return jnp.stack([cos, sin, cos * QK_SCALE, sin * QK_SCALE])


def _permute_swa_heads(t, axis):
    shape = t.shape
    t = t.reshape(shape[:axis] + (SWA_Q_HEADS, HEAD_DIM) + shape[axis + 1:])
    t = jnp.take(t, jnp.asarray(SWA_HEAD_ORDER), axis=axis)
    return t.reshape(shape)


def _reorder_w_in(w_in):
    edges = np.cumsum([0, SWA_Q_W, SWA_KV_W, SWA_KV_W, SB_W, SB_W, SB_W, RET_W, RET_W, RET_W, RET_W])
    qa, ka, va, qb, kb, vb, qc, kc, vc, gc = [w_in[..., a:e] for a, e in zip(edges[:-1], edges[1:])]
    return jnp.concatenate([_permute_swa_heads(qa, 2), qc, kc, vc, ka, va, qb, kb, vb, gc], axis=-1)


def kernel(x, positions, w_in, w_out, sinks, branch_gain, w_up, w_down,
           norm_mix_pre, norm_mix_post, norm_mlp_pre, norm_mlp_post):
    b, s, d = x.shape
    depth = w_in.shape[0]
    rope_tab = _rope_table(positions)
    w_in = _reorder_w_in(w_in).astype(jnp.bfloat16)
    w_out = jnp.concatenate([_permute_swa_heads(w_out[:, :SWA_Q_W], 1), w_out[:, SWA_Q_W:]],
                            axis=1).astype(jnp.bfloat16)
    gain_a = _permute_swa_heads(branch_gain[:, :SWA_Q_W], 1)
    w_up = w_up.astype(jnp.bfloat16)
    w_down = w_down.astype(jnp.bfloat16)
    row = lambda t: t.reshape(1, -1)

    x2d = x.reshape(b * s, d)
    for l in range(depth):
        proj, gate = _in_proj(x2d, row(norm_mix_pre[l]), w_in[l], rope_tab, s)
        out_a, out_b, out_c = _mixers(sinks[l], proj.reshape(b, s, PROJ_W), gate.reshape(b, s, RET_W),
                                      row(branch_gain[l, SWA_Q_W + SB_W:]))
        x2d = _out_mlp(out_a.reshape(b * s, SWA_Q_W), out_b.reshape(b * s, SB_W),
                       out_c.reshape(b * s, RET_W), x2d,
                       row(gain_a[l]), row(branch_gain[l, SWA_Q_W:SWA_Q_W + SB_W]),
                       row(norm_mix_post[l]), row(norm_mlp_pre[l]), row(norm_mlp_post[l]),
                       w_out[l], w_up[l], w_down[l])
    return x2d.reshape(b, s, d)
```

```python
import functools
import itertools

import numpy as np
import jax
import jax.numpy as jnp
from jax import lax
from jax.experimental import pallas as pl
from jax.experimental.pallas import tpu as pltpu

D_MODEL = 1024
HEAD_DIM = 64
SWA_Q_HEADS = 6
SWA_KV_HEADS = 2
SB_HEADS = 4
RET_HEADS = 6
BLOCK = 128
ROPE_THETA = 10000.0
D_FF = 4 * D_MODEL
NORM_EPS = 1e-6
LANES = 128

SWA_Q_W = SWA_Q_HEADS * HEAD_DIM
SWA_KV_W = SWA_KV_HEADS * HEAD_DIM
SB_W = SB_HEADS * HEAD_DIM
RET_W = RET_HEADS * HEAD_DIM
IN_W = SWA_Q_W + 2 * SWA_KV_W + 3 * SB_W + 4 * RET_W
PROJ_W = IN_W - RET_W

QA_BLK = 0
QC_BLK, KC_BLK, VC_BLK = 3, 6, 9
KA_BLK, VA_BLK = 12, 13
QB_BLK, KB_BLK, VB_BLK = 14, 16, 18
GATE_BLK = 20
QK_SCALE = HEAD_DIM ** -0.5

SWA_HEAD_ORDER = (0, 3, 1, 4, 2, 5)

SB_DEAD_SUM = 104.0
SB_FUSED_BLOCKS = 3

TOKEN_TILE = 512
VMEM_LIMIT = 62 * 1024 * 1024


def _rms(xf, gain):
    return xf * lax.rsqrt(jnp.mean(xf * xf, axis=-1, keepdims=True) + NORM_EPS) * gain


def _low_half_mask(shape):
    return lax.broadcasted_iota(jnp.int32, shape, len(shape) - 1) < HEAD_DIM


def _stack_heads(pair):
    low = _low_half_mask(pair.shape)
    zero = jnp.zeros_like(pair)
    return jnp.concatenate([jnp.where(low, pair, zero), jnp.where(low, zero, pair)], axis=0)


def _dot_nt(a, b):
    return lax.dot_general(a, b, (((1,), (1,)), ((), ())), preferred_element_type=jnp.float32)


def _dot(a, b):
    return jnp.dot(a, b, preferred_element_type=jnp.float32)


def _in_blk_kind(blk):
    if QA_BLK <= blk < QC_BLK or KC_BLK <= blk < VC_BLK:
        return 1, 1.0
    if QC_BLK <= blk < KC_BLK or blk == KA_BLK:
        return 0, 1.0
    if QB_BLK <= blk < KB_BLK:
        return None, QK_SCALE
    return None, 1.0


def _in_proj_kernel(x_ref, g_ref, w_ref, rope_ref, proj_ref, gate_ref):
    hn = _rms(x_ref[...], g_ref[...]).astype(jnp.bfloat16)
    first_half = (lax.broadcasted_iota(jnp.int32, (x_ref.shape[0], LANES), 1) % HEAD_DIM) < HEAD_DIM // 2
    chunk = 2 * LANES
    for c0 in range(0, IN_W, chunk):
        width = min(chunk, IN_W - c0)
        p = _dot(hn, w_ref[:, c0:c0 + width])
        for s0 in range(0, width, LANES):
            blk = (c0 + s0) // LANES
            pb = p[:, s0:s0 + LANES]
            table, scale = _in_blk_kind(blk)
            if table is not None:
                cos = rope_ref[2 * table]
                sin = rope_ref[2 * table + 1]
                rot = jnp.where(first_half, pltpu.roll(pb, LANES - HEAD_DIM // 2, 1),
                                pltpu.roll(pb, HEAD_DIM // 2, 1))
                pb = pb * cos + rot * sin
            elif scale != 1.0:
                pb = pb * scale
            if blk >= GATE_BLK:
                g0 = (blk - GATE_BLK) * LANES
                gate_ref[:, g0:g0 + LANES] = pb
            else:
                proj_ref[:, blk * LANES:(blk + 1) * LANES] = pb.astype(jnp.bfloat16)


def _in_proj(x2d, gain, w_bf16, rope_tab, seq):
    n_tok = x2d.shape[0]
    tm = min(TOKEN_TILE, seq)
    seq_tiles = seq // tm
    const = lambda i: (0, 0)
    return pl.pallas_call(
        _in_proj_kernel,
        grid=(n_tok // tm,),
        in_specs=[
            pl.BlockSpec((tm, D_MODEL), lambda i: (i, 0)),
            pl.BlockSpec((1, D_MODEL), const),
            pl.BlockSpec((D_MODEL, IN_W), const, pipeline_mode=pl.Buffered(1)),
            pl.BlockSpec((4, tm, LANES), lambda i: (0, i % seq_tiles, 0)),
        ],
        out_specs=[
            pl.BlockSpec((tm, PROJ_W), lambda i: (i, 0)),
            pl.BlockSpec((tm, RET_W), lambda i: (i, 0)),
        ],
        out_shape=[
            jax.ShapeDtypeStruct((n_tok, PROJ_W), jnp.bfloat16),
            jax.ShapeDtypeStruct((n_tok, RET_W), jnp.float32),
        ],
        compiler_params=pltpu.CompilerParams(
            dimension_semantics=("arbitrary",), vmem_limit_bytes=VMEM_LIMIT),
        name="in_proj",
    )(x2d, gain, w_bf16, rope_tab)


def _swa_stages(i, sink_ref, q_ref, k_ref, kp_ref, v_ref, vp_ref, out):
    n_blocks = q_ref.shape[1] // BLOCK
    rows = lax.broadcasted_iota(jnp.int32, (BLOCK, 2 * BLOCK), 0)
    cols = lax.broadcasted_iota(jnp.int32, (BLOCK, 2 * BLOCK), 1)
    own = (cols % BLOCK) <= rows
    low_lanes = _low_half_mask((BLOCK, LANES))
    first_bias = jnp.where(i > 0, 0.0, -jnp.inf).astype(jnp.float32)
    ks = [_stack_heads(kp_ref[0])] + [_stack_heads(k_ref[0, n * BLOCK:(n + 1) * BLOCK, :]) for n in range(n_blocks)]
    vs = [_stack_heads(vp_ref[0])] + [_stack_heads(v_ref[0, n * BLOCK:(n + 1) * BLOCK, :]) for n in range(n_blocks)]
    tiles = [(n, c) for n in range(n_blocks) for c in range(SWA_Q_HEADS // 2)]
    scores = {}
    for n, c in tiles:
        q = q_ref[0, n * BLOCK:(n + 1) * BLOCK, c * LANES:(c + 1) * LANES]
        s_prev = _dot_nt(q, ks[n])
        if n == 0:
            s_prev = s_prev + first_bias
        scores[n, c] = jnp.where(own, _dot_nt(q, ks[n + 1]), s_prev)
    yield
    weights = {}
    for n, c in tiles:
        probs, scales = [], []
        for half, head in enumerate(SWA_HEAD_ORDER[2 * c:2 * c + 2]):
            sh = scores[n, c][:, half * BLOCK:(half + 1) * BLOCK]
            sink = sink_ref[head]
            m = jnp.maximum(jnp.max(sh, axis=-1, keepdims=True), sink)
            p = jnp.exp(sh - m)
            probs.append(p)
            scales.append(1.0 / (jnp.sum(p, axis=-1, keepdims=True) + jnp.exp(sink - m)))
        p = jnp.concatenate(probs, axis=1)
        weights[n, c] = (jnp.where(own, p, 0.0).astype(jnp.bfloat16), jnp.where(own, 0.0, p).astype(jnp.bfloat16),
                         jnp.where(low_lanes, scales[0], scales[1]))
    for n, c in tiles:
        p_own, p_prev, scale = weights[n, c]
        o = _dot(p_own, vs[n + 1]) + _dot(p_prev, vs[n])
        out.store(n * BLOCK, c * LANES, o * scale)
    yield


class _StickBreaking:
    def __init__(self, i, q_ref, k_ref, v_ref, out):
        self.k_ref, self.v_ref, self.out = k_ref, v_ref, out
        self.subs = q_ref.shape[1] // BLOCK
        self.first = i * self.subs
        pairs = SB_HEADS // 2
        wide = 2 * BLOCK
        self.units = [(u, p) for u in range(self.subs) for p in range(pairs)]
        self.q = {(u, p): q_ref[0, u * BLOCK:(u + 1) * BLOCK, p * LANES:(p + 1) * LANES] for u, p in self.units}
        rows = lax.broadcasted_iota(jnp.int32, (BLOCK, wide), 0)
        cols = lax.broadcasted_iota(jnp.int32, (BLOCK, wide), 1)
        self.strict = (cols % BLOCK) < rows
        kj = lax.broadcasted_iota(jnp.int32, (wide, wide), 0)
        ks = lax.broadcasted_iota(jnp.int32, (wide, wide), 1)
        self.tail_mat = jnp.where(((kj < BLOCK) == (ks < BLOCK)) & (kj > ks), 1.0, 0.0).astype(jnp.bfloat16)
        self.carry = {key: (jnp.zeros((BLOCK, 1), jnp.float32),) * 2 for key in self.units}
        self.acc = {key: jnp.zeros((BLOCK, LANES), jnp.float32) for key in self.units}

    def _load(self, ref, p, j):
        start = pl.multiple_of(j * BLOCK, BLOCK)
        return _stack_heads(ref[0, pl.ds(start, BLOCK), p * LANES:(p + 1) * LANES])

    @staticmethod
    def _bias_if(valid):
        return jnp.where(valid, 0.0, -jnp.inf).astype(jnp.float32)

    def _tiles(self, units, jobs, carry, acc, out):
        keys, vals = {}, {}
        for u, p in units:
            for j, _ in jobs[u]:
                if (p, id(j)) not in keys:
                    keys[p, id(j)], vals[p, id(j)] = self._load(self.k_ref, p, j), self._load(self.v_ref, p, j)
        scores = {(u, p, t): _dot_nt(self.q[u, p], keys[p, id(j)])
                  for u, p in units for t, (j, _) in enumerate(jobs[u])}
        yield
        log_beta, tails, row_sums = {}, {}, {}
        for key, z in scores.items():
            _, mask = jobs[key[0]][key[2]]
            if isinstance(mask, str):
                z = jnp.where(self.strict, z, -jnp.inf)
            elif mask is not None:
                z = z + mask
            sp = jnp.maximum(z, 0.0) + jnp.log(1.0 + jnp.exp(-jnp.abs(z)))
            log_beta[key] = z - sp
            tails[key] = _dot(sp.astype(jnp.bfloat16), self.tail_mat)
            row_sums[key] = (jnp.sum(sp[:, :BLOCK], axis=-1, keepdims=True),
                             jnp.sum(sp[:, BLOCK:], axis=-1, keepdims=True))
        yield
        new_carry, new_acc = {}, {}
        for u, p in units:
            c = carry[u, p]
            ws = []
            for t in range(len(jobs[u])):
                key = (u, p, t)
                logits = log_beta[key] - tails[key]
                ws += [jnp.exp(logits[:, h * BLOCK:(h + 1) * BLOCK] - c[h]).astype(jnp.bfloat16) for h in range(2)]
                c = tuple(c[h] + row_sums[key][h] for h in range(2))
            new_carry[u, p] = c
            v_cat = jnp.concatenate([vals[p, id(j)] for j, _ in jobs[u]], axis=0)
            new_acc[u, p] = acc[u, p] + _dot(jnp.concatenate(ws, axis=1), v_cat)
        out["carry"], out["acc"] = new_carry, new_acc
        yield

    def fused_stages(self, wave):
        first, subs = self.first, self.subs
        block_at = {d: jnp.maximum(first + d, 0) for d in range(1 - SB_FUSED_BLOCKS, subs)}
        jobs = [[(block_at[u - back],
                  "diag" if back == 0 else (None if u >= back else self._bias_if(first + u - back >= 0)))
                 for back in range(SB_FUSED_BLOCKS)] for u in range(subs)]
        half = len(self.units) // 2
        out = {}
        yield from self._tiles(self.units[wave * half:(wave + 1) * half], jobs, self.carry, self.acc, out)
        self.carry.update(out["carry"])
        self.acc.update(out["acc"])

    def finish(self):
        first, subs, units = self.first, self.subs, self.units

        def alive_after(cs):
            least = functools.reduce(jnp.minimum, [c for key in units for c in cs[key]])
            return (jnp.min(least) < SB_DEAD_SUM).astype(jnp.int32)

        flat = lambda d: tuple(d[key] for key in units)
        unflat = lambda t: dict(zip(units, t))

        def cond(state):
            back, alive, _, _ = state
            return jnp.logical_and(first + subs - 1 - back >= 0, alive > 0)

        def body(state):
            back, _, cs, accs = state
            jobs = [[(jnp.maximum(first + u - back, 0),
                      None if u == subs - 1 else self._bias_if(first + u - back >= 0))] for u in range(subs)]
            out = {}
            for _ in self._tiles(units, jobs, unflat(cs), unflat(accs), out):
                pass
            return back + 1, alive_after(out["carry"]), flat(out["carry"]), flat(out["acc"])

        state = (jnp.int32(SB_FUSED_BLOCKS), alive_after(self.carry), flat(self.carry), flat(self.acc))
        acc = unflat(lax.while_loop(cond, body, state)[3])
        for u, p in units:
            self.out.store(u * BLOCK, p * LANES, acc[u, p])


def _ret_stages(q_ref, k_ref, v_ref, gate_ref, gain_ref, decay_ref, qdec_ref, kdec_ref, sdec_ref,
                out, state_ref):
    n_chunks = q_ref.shape[1] // BLOCK
    low = _low_half_mask((BLOCK, LANES))
    same_head = (lax.broadcasted_iota(jnp.int32, (LANES, LANES), 0) < HEAD_DIM) == _low_half_mask((LANES, LANES))

    def head_mean(t):
        lo_sum = jnp.sum(jnp.where(low, t, 0.0), axis=-1, keepdims=True)
        hi_sum = jnp.sum(jnp.where(low, 0.0, t), axis=-1, keepdims=True)
        return jnp.where(low, lo_sum, hi_sum) * (1.0 / HEAD_DIM)

    tiles = [(p, n) for p in range(RET_HEADS // 2) for n in range(n_chunks)]
    window = lambda ref, p, n: ref[0, n * BLOCK:(n + 1) * BLOCK, p * LANES:(p + 1) * LANES]
    q = {t: window(q_ref, *t) for t in tiles}
    k = {t: window(k_ref, *t) for t in tiles}
    vs = {t: _stack_heads(window(v_ref, *t)) for t in tiles}
    scores = {t: _dot_nt(q[t], _stack_heads(k[t])) for t in tiles}
    kv = {}
    for t in tiles:
        kd = (k[t].astype(jnp.float32) * kdec_ref[t[0]]).astype(jnp.bfloat16)
        kv[t] = lax.dot_general(kd, window(v_ref, *t), (((0,), (0,)), ((), ())),
                                preferred_element_type=jnp.float32)
    yield
    o_intra = {t: _dot((scores[t] * decay_ref[t[0]]).astype(jnp.bfloat16), vs[t]) for t in tiles}
    yield
    o = {}
    for p in range(RET_HEADS // 2):
        state = state_ref[p]
        for n in range(n_chunks):
            qd = (q[p, n].astype(jnp.float32) * qdec_ref[p]).astype(jnp.bfloat16)
            o[p, n] = o_intra[p, n] + _dot(qd, state.astype(jnp.bfloat16))
            state = sdec_ref[p] * state + jnp.where(same_head, kv[p, n], 0.0)
        state_ref[p] = state
    for t in tiles:
        cen = o[t] - head_mean(o[t])
        var = head_mean(cen * cen)
        normed = cen * lax.rsqrt(var + NORM_EPS) * gain_ref[:, t[0] * LANES:(t[0] + 1) * LANES]
        gate = window(gate_ref, *t)
        out.store(t[1] * BLOCK, t[0] * LANES, gate * jax.nn.sigmoid(gate) * normed)
    yield


class _Columns:
    def __init__(self, ref, col0):
        self.ref, self.col0 = ref, col0

    def store(self, r0, c0, value):
        rows, cols = value.shape
        self.ref[r0:r0 + rows, self.col0 + c0:self.col0 + c0 + cols] = value


FF_CHUNK = 512
MLP_CHUNK_GROUPS = (1, 2, 1, 2, 2)


def _mlp_stages(mix_ref, x_ref, ga_ref, gb_ref, gpost_ref, gpre_ref, gmlp_ref, wout_ref, wup_ref, wdown_ref,
                o_ref):
    na = _rms(mix_ref[:, 0:SWA_Q_W], ga_ref[...]).astype(jnp.bfloat16)
    nb = _rms(mix_ref[:, SWA_Q_W:SWA_Q_W + SB_W], gb_ref[...]).astype(jnp.bfloat16)
    nc = mix_ref[:, SWA_Q_W + SB_W:].astype(jnp.bfloat16)
    yield
    y = (_dot(na, wout_ref[0:SWA_Q_W, :]) + _dot(nb, wout_ref[SWA_Q_W:SWA_Q_W + SB_W, :])
         + _dot(nc, wout_ref[SWA_Q_W + SB_W:, :]))
    x1 = x_ref[...] + _rms(y, gpost_ref[...])
    hn = _rms(x1, gpre_ref[...]).astype(jnp.bfloat16)
    acc = jnp.zeros_like(x1)
    f0 = 0
    for group in MLP_CHUNK_GROUPS:
        for _ in range(group):
            u = jnp.maximum(_dot(hn, wup_ref[:, f0:f0 + FF_CHUNK]), 0.0)
            acc = acc + _dot((u * u).astype(jnp.bfloat16), wdown_ref[f0:f0 + FF_CHUNK, :])
            f0 += FF_CHUNK
        yield
    assert f0 == D_FF
    o_ref[...] = x1 + _rms(acc, gmlp_ref[...])
    yield


def _layer_kernel(sink_ref, qa_ref, ka_ref, kap_ref, va_ref, vap_ref, qb_ref, kb_ref, vb_ref,
                  qc_ref, kc_ref, vc_ref, gate_ref, gain_ref, decay_ref, qdec_ref, kdec_ref, sdec_ref,
                  x_ref, ga_ref, gb_ref, gpost_ref, gpre_ref, gmlp_ref, wout_ref, wup_ref, wdown_ref,
                  o_ref, state_ref, mix_ref, *, seq_tiles, n_tiles):
    s = pl.program_id(0)
    i = jnp.minimum(s, n_tiles - 1) % seq_tiles

    @pl.when(s == 0)
    def _():
        mix_ref[...] = jnp.zeros_like(mix_ref)

    @pl.when(i == 0)
    def _():
        state_ref[...] = jnp.zeros_like(state_ref)

    sb = _StickBreaking(i, qb_ref, kb_ref, vb_ref, _Columns(mix_ref, SWA_Q_W))
    stages = [_mlp_stages(mix_ref, x_ref, ga_ref, gb_ref, gpost_ref, gpre_ref, gmlp_ref, wout_ref, wup_ref,
                          wdown_ref, o_ref),
              itertools.chain(sb.fused_stages(0), sb.fused_stages(1)),
              _swa_stages(i, sink_ref, qa_ref, ka_ref, kap_ref, va_ref, vap_ref, _Columns(mix_ref, 0)),
              _ret_stages(qc_ref, kc_ref, vc_ref, gate_ref, gain_ref, decay_ref, qdec_ref, kdec_ref, sdec_ref,
                          _Columns(mix_ref, SWA_Q_W + SB_W), state_ref)]
    for _ in itertools.zip_longest(*stages):
        pass
    sb.finish()


def _retention_tables():
    h = jnp.arange(RET_HEADS, dtype=jnp.float32)
    log_gamma = jnp.log1p(-(2.0 ** (-5.0 - h)))
    pos = jnp.arange(BLOCK, dtype=jnp.float32)
    rel = pos[:, None] - pos[None, :]
    decay = jnp.where(rel[None] >= 0,
                      jnp.exp(jnp.maximum(rel, 0.0)[None] * log_gamma[:, None, None]), 0.0)
    decay = jnp.swapaxes(decay.reshape(RET_HEADS // 2, 2, BLOCK, BLOCK), 1, 2)
    decay = decay.reshape(RET_HEADS // 2, BLOCK, 2 * BLOCK)
    per_lane = lambda t: jnp.repeat(t.reshape(-1, RET_HEADS // 2, 2), HEAD_DIM, axis=2)
    to_pairs = lambda t: jnp.moveaxis(per_lane(t).reshape(-1, RET_HEADS // 2, LANES), 1, 0)
    q_dec = to_pairs(jnp.exp((pos + 1.0)[:, None] * log_gamma[None, :]))
    k_dec = to_pairs(jnp.exp((BLOCK - 1 - pos)[:, None] * log_gamma[None, :]))
    chunk_decay = to_pairs(jnp.exp(BLOCK * log_gamma)[None, :])
    s_dec = jnp.broadcast_to(jnp.swapaxes(chunk_decay, 1, 2), (RET_HEADS // 2, LANES, LANES))
    return decay, q_dec, k_dec, s_dec


def _mix_mlp(sinks, proj, gate, gn_gain, x2d, ga, gb, g_post, g_pre, g_mlp, w_out, w_up, w_down):
    b, s, _ = proj.shape
    tt = min(TOKEN_TILE, s)
    per = tt // BLOCK
    seq_tiles = s // tt
    n_tiles = b * seq_tiles
    tables = _retention_tables()
    mixer_tile = lambda g: jnp.minimum(g, n_tiles - 1)
    bi = lambda g: mixer_tile(g) // seq_tiles
    ti = lambda g: mixer_tile(g) % seq_tiles
    tile = lambda width, blk: pl.BlockSpec((1, tt, width), lambda g: (bi(g), ti(g), blk * LANES // width))
    prev = lambda blk: pl.BlockSpec((1, BLOCK, LANES), lambda g: (bi(g), jnp.maximum(ti(g) * per - 1, 0), blk))
    whole = lambda width, blk: pl.BlockSpec((1, s, width), lambda g: (bi(g), 0, blk * LANES // width),
                                            pipeline_mode=pl.Buffered(1))
    const = lambda shape: pl.BlockSpec(shape, lambda g: (0,) * len(shape))
    weight = lambda shape: pl.BlockSpec(shape, lambda g: (0, 0), pipeline_mode=pl.Buffered(1))
    mlp_rows = pl.BlockSpec((tt, D_MODEL), lambda g: (jnp.maximum(g - 1, 0), 0))
    return pl.pallas_call(
        functools.partial(_layer_kernel, seq_tiles=seq_tiles, n_tiles=n_tiles),
        grid=(n_tiles + 1,),
        in_specs=[
            pl.BlockSpec(memory_space=pltpu.SMEM),
            tile(SWA_Q_W, QA_BLK), tile(LANES, KA_BLK), prev(KA_BLK), tile(LANES, VA_BLK), prev(VA_BLK),
            tile(SB_W, QB_BLK), whole(SB_W, KB_BLK), whole(SB_W, VB_BLK),
            tile(RET_W, QC_BLK), tile(RET_W, KC_BLK), tile(RET_W, VC_BLK), tile(RET_W, 0),
            const((1, RET_W)),
        ] + [const(t.shape) for t in tables] + [
            mlp_rows,
            const((1, SWA_Q_W)), const((1, SB_W)), const((1, D_MODEL)), const((1, D_MODEL)), const((1, D_MODEL)),
            weight((D_MODEL, D_MODEL)), weight((D_MODEL, D_FF)), weight((D_FF, D_MODEL)),
        ],
        out_specs=mlp_rows,
        out_shape=jax.ShapeDtypeStruct(x2d.shape, jnp.float32),
        scratch_shapes=[pltpu.VMEM((RET_HEADS // 2, LANES, LANES), jnp.float32),
                        pltpu.VMEM((tt, D_MODEL), jnp.float32)],
        compiler_params=pltpu.CompilerParams(
            dimension_semantics=("arbitrary",), vmem_limit_bytes=VMEM_LIMIT),
        name="mix_mlp",
    )(sinks, proj, proj, proj, proj, proj, proj, proj, proj, proj, proj, proj, gate, gn_gain, *tables,
      x2d, ga, gb, g_post, g_pre, g_mlp, w_out, w_up, w_down)


def _rope_table(positions):
    inv_freq = ROPE_THETA ** (-jnp.arange(0, HEAD_DIM, 2, dtype=jnp.float32) / HEAD_DIM)
    ang = positions.astype(jnp.float32)[:, None] * inv_freq[None, :]
    cos = jnp.tile(jnp.cos(ang), (1, 4))
    sin = jnp.tile(jnp.concatenate([-jnp.sin(ang), jnp.sin(ang)], axis=1), (1, 2))
    return jnp.stack([cos, sin, cos * QK_SCALE, sin * QK_SCALE])


def _permute_swa_heads(t, axis):
    shape = t.shape
    t = t.reshape(shape[:axis] + (SWA_Q_HEADS, HEAD_DIM) + shape[axis + 1:])
    t = jnp.take(t, jnp.asarray(SWA_HEAD_ORDER), axis=axis)
    return t.reshape(shape)


def _reorder_w_in(w_in):
    edges = np.cumsum([0, SWA_Q_W, SWA_KV_W, SWA_KV_W, SB_W, SB_W, SB_W, RET_W, RET_W, RET_W, RET_W])
    qa, ka, va, qb, kb, vb, qc, kc, vc, gc = [w_in[..., a:e] for a, e in zip(edges[:-1], edges[1:])]
    return jnp.concatenate([_permute_swa_heads(qa, 2), qc, kc, vc, ka, va, qb, kb, vb, gc], axis=-1)


def kernel(x, positions, w_in, w_out, sinks, branch_gain, w_up, w_down,
           norm_mix_pre, norm_mix_post, norm_mlp_pre, norm_mlp_post):
    b, s, d = x.shape
    depth = w_in.shape[0]
    rope_tab = _rope_table(positions)
    w_in = _reorder_w_in(w_in).astype(jnp.bfloat16)
    w_out = jnp.concatenate([_permute_swa_heads(w_out[:, :SWA_Q_W], 1), w_out[:, SWA_Q_W:]],
                            axis=1).astype(jnp.bfloat16)
    gain_a = _permute_swa_heads(branch_gain[:, :SWA_Q_W], 1)
    w_up = w_up.astype(jnp.bfloat16)
    w_down = w_down.astype(jnp.bfloat16)
    row = lambda t: t.reshape(1, -1)

    x2d = x.reshape(b * s, d)
    for l in range(depth):
        proj, gate = _in_proj(x2d, row(norm_mix_pre[l]), w_in[l], rope_tab, s)
        x2d = _mix_mlp(sinks[l], proj.reshape(b, s, PROJ_W), gate.reshape(b, s, RET_W),
                       row(branch_gain[l, SWA_Q_W + SB_W:]), x2d,
                       row(gain_a[l]), row(branch_gain[l, SWA_Q_W:SWA_Q_W + SB_W]),
                       row(norm_mix_post[l]), row(norm_mlp_pre[l]), row(norm_mlp_post[l]),
                       w_out[l], w_up[l], w_down[l])
    return x2d.reshape(b, s, d)
```

```python
import functools
import itertools

import numpy as np
import jax
import jax.numpy as jnp
from jax import lax
from jax.experimental import pallas as pl
from jax.experimental.pallas import tpu as pltpu

D_MODEL = 1024
HEAD_DIM = 64
SWA_Q_HEADS = 6
SWA_KV_HEADS = 2
SB_HEADS = 4
RET_HEADS = 6
BLOCK = 128
ROPE_THETA = 10000.0
D_FF = 4 * D_MODEL
NORM_EPS = 1e-6
LANES = 128

SWA_Q_W = SWA_Q_HEADS * HEAD_DIM
SWA_KV_W = SWA_KV_HEADS * HEAD_DIM
SB_W = SB_HEADS * HEAD_DIM
RET_W = RET_HEADS * HEAD_DIM
IN_W = SWA_Q_W + 2 * SWA_KV_W + 3 * SB_W + 4 * RET_W
PROJ_W = IN_W - RET_W

QA_BLK = 0
QC_BLK, KC_BLK, VC_BLK = 3, 6, 9
KA_BLK, VA_BLK = 12, 13
QB_BLK, KB_BLK, VB_BLK = 14, 16, 18
GATE_BLK = 20
QK_SCALE = HEAD_DIM ** -0.5

SWA_HEAD_ORDER = (0, 3, 1, 4, 2, 5)

SB_DEAD_SUM = 104.0
SB_FUSED_BLOCKS = 3

TOKEN_TILE = 512
VMEM_LIMIT = 62 * 1024 * 1024


def _rms(xf, gain):
    return xf * lax.rsqrt(jnp.mean(xf * xf, axis=-1, keepdims=True) + NORM_EPS) * gain


def _low_half_mask(shape):
    return lax.broadcasted_iota(jnp.int32, shape, len(shape) - 1) < HEAD_DIM


def _stack_heads(pair):
    low = _low_half_mask(pair.shape)
    zero = jnp.zeros_like(pair)
    return jnp.concatenate([jnp.where(low, pair, zero), jnp.where(low, zero, pair)], axis=0)


def _dot_nt(a, b):
    return lax.dot_general(a, b, (((1,), (1,)), ((), ())), preferred_element_type=jnp.float32)


def _dot(a, b):
    return jnp.dot(a, b, preferred_element_type=jnp.float32)


def _in_blk_kind(blk):
    if QA_BLK <= blk < QC_BLK or KC_BLK <= blk < VC_BLK:
        return 1, 1.0
    if QC_BLK <= blk < KC_BLK or blk == KA_BLK:
        return 0, 1.0
    if QB_BLK <= blk < KB_BLK:
        return None, QK_SCALE
    return None, 1.0


def _in_proj_kernel(x_ref, g_ref, w_ref, rope_ref, proj_ref, gate_ref):
    hn = _rms(x_ref[...], g_ref[...]).astype(jnp.bfloat16)
    first_half = (lax.broadcasted_iota(jnp.int32, (x_ref.shape[0], LANES), 1) % HEAD_DIM) < HEAD_DIM // 2
    chunk = 2 * LANES
    for c0 in range(0, IN_W, chunk):
        width = min(chunk, IN_W - c0)
        p = _dot(hn, w_ref[:, c0:c0 + width])
        for s0 in range(0, width, LANES):
            blk = (c0 + s0) // LANES
            pb = p[:, s0:s0 + LANES]
            table, scale = _in_blk_kind(blk)
            if table is not None:
                cos = rope_ref[2 * table]
                sin = rope_ref[2 * table + 1]
                rot = jnp.where(first_half, pltpu.roll(pb, LANES - HEAD_DIM // 2, 1),
                                pltpu.roll(pb, HEAD_DIM // 2, 1))
                pb = pb * cos + rot * sin
            elif scale != 1.0:
                pb = pb * scale
            if blk >= GATE_BLK:
                g0 = (blk - GATE_BLK) * LANES
                gate_ref[:, g0:g0 + LANES] = pb
            else:
                proj_ref[:, blk * LANES:(blk + 1) * LANES] = pb.astype(jnp.bfloat16)


def _in_proj(x2d, gain, w_in, layer, rope_tab, seq):
    n_tok = x2d.shape[0]
    tm = min(TOKEN_TILE, seq)
    seq_tiles = seq // tm
    const = lambda i: (0, 0)
    return pl.pallas_call(
        _in_proj_kernel,
        grid=(n_tok // tm,),
        in_specs=[
            pl.BlockSpec((tm, D_MODEL), lambda i: (i, 0)),
            pl.BlockSpec((1, D_MODEL), const),
            pl.BlockSpec((None, D_MODEL, IN_W), lambda i: (layer, 0, 0), pipeline_mode=pl.Buffered(1)),
            pl.BlockSpec((4, tm, LANES), lambda i: (0, i % seq_tiles, 0)),
        ],
        out_specs=[
            pl.BlockSpec((tm, PROJ_W), lambda i: (i, 0)),
            pl.BlockSpec((tm, RET_W), lambda i: (i, 0)),
        ],
        out_shape=[
            jax.ShapeDtypeStruct((n_tok, PROJ_W), jnp.bfloat16),
            jax.ShapeDtypeStruct((n_tok, RET_W), jnp.float32),
        ],
        compiler_params=pltpu.CompilerParams(
            dimension_semantics=("arbitrary",), vmem_limit_bytes=VMEM_LIMIT),
        name="in_proj",
    )(x2d, gain, w_in, rope_tab)


def _swa_stages(i, sink_ref, q_ref, k_ref, kp_ref, v_ref, vp_ref, out):
    n_blocks = q_ref.shape[1] // BLOCK
    rows = lax.broadcasted_iota(jnp.int32, (BLOCK, 2 * BLOCK), 0)
    cols = lax.broadcasted_iota(jnp.int32, (BLOCK, 2 * BLOCK), 1)
    own = (cols % BLOCK) <= rows
    low_lanes = _low_half_mask((BLOCK, LANES))
    first_bias = jnp.where(i > 0, 0.0, -jnp.inf).astype(jnp.float32)
    ks = [_stack_heads(kp_ref[0])] + [_stack_heads(k_ref[0, n * BLOCK:(n + 1) * BLOCK, :]) for n in range(n_blocks)]
    vs = [_stack_heads(vp_ref[0])] + [_stack_heads(v_ref[0, n * BLOCK:(n + 1) * BLOCK, :]) for n in range(n_blocks)]
    tiles = [(n, c) for n in range(n_blocks) for c in range(SWA_Q_HEADS // 2)]
    scores = {}
    for n, c in tiles:
        q = q_ref[0, n * BLOCK:(n + 1) * BLOCK, c * LANES:(c + 1) * LANES]
        s_prev = _dot_nt(q, ks[n])
        if n == 0:
            s_prev = s_prev + first_bias
        scores[n, c] = jnp.where(own, _dot_nt(q, ks[n + 1]), s_prev)
    yield
    weights = {}
    for n, c in tiles:
        probs, scales = [], []
        for half, head in enumerate(SWA_HEAD_ORDER[2 * c:2 * c + 2]):
            sh = scores[n, c][:, half * BLOCK:(half + 1) * BLOCK]
            sink = sink_ref[head]
            m = jnp.maximum(jnp.max(sh, axis=-1, keepdims=True), sink)
            p = jnp.exp(sh - m)
            probs.append(p)
            scales.append(1.0 / (jnp.sum(p, axis=-1, keepdims=True) + jnp.exp(sink - m)))
        p = jnp.concatenate(probs, axis=1)
        weights[n, c] = (jnp.where(own, p, 0.0).astype(jnp.bfloat16), jnp.where(own, 0.0, p).astype(jnp.bfloat16),
                         jnp.where(low_lanes, scales[0], scales[1]))
    yield
    for n, c in tiles:
        p_own, p_prev, scale = weights[n, c]
        o = _dot(p_own, vs[n + 1]) + _dot(p_prev, vs[n])
        out.store(n * BLOCK, c * LANES, o * scale)
    yield


class _StickBreaking:
    def __init__(self, i, q_ref, k_ref, v_ref, out):
        self.k_ref, self.v_ref, self.out = k_ref, v_ref, out
        self.subs = q_ref.shape[1] // BLOCK
        self.first = i * self.subs
        pairs = SB_HEADS // 2
        wide = 2 * BLOCK
        self.units = [(u, p) for u in range(self.subs) for p in range(pairs)]
        self.q = {(u, p): q_ref[0, u * BLOCK:(u + 1) * BLOCK, p * LANES:(p + 1) * LANES] for u, p in self.units}
        rows = lax.broadcasted_iota(jnp.int32, (BLOCK, wide), 0)
        cols = lax.broadcasted_iota(jnp.int32, (BLOCK, wide), 1)
        self.strict = (cols % BLOCK) < rows
        kj = lax.broadcasted_iota(jnp.int32, (wide, wide), 0)
        ks = lax.broadcasted_iota(jnp.int32, (wide, wide), 1)
        self.tail_mat = jnp.where(((kj < BLOCK) == (ks < BLOCK)) & (kj > ks), 1.0, 0.0).astype(jnp.bfloat16)
        self.carry = {key: (jnp.zeros((BLOCK, 1), jnp.float32),) * 2 for key in self.units}
        self.acc = {key: jnp.zeros((BLOCK, LANES), jnp.float32) for key in self.units}

    def _load(self, ref, p, j):
        start = pl.multiple_of(j * BLOCK, BLOCK)
        return _stack_heads(ref[0, pl.ds(start, BLOCK), p * LANES:(p + 1) * LANES])

    @staticmethod
    def _bias_if(valid):
        return jnp.where(valid, 0.0, -jnp.inf).astype(jnp.float32)

    def _tiles(self, units, jobs, carry, acc, out):
        keys, vals = {}, {}
        for u, p in units:
            for j, _ in jobs[u]:
                if (p, id(j)) not in keys:
                    keys[p, id(j)], vals[p, id(j)] = self._load(self.k_ref, p, j), self._load(self.v_ref, p, j)
        scores = {(u, p, t): _dot_nt(self.q[u, p], keys[p, id(j)])
                  for u, p in units for t, (j, _) in enumerate(jobs[u])}
        yield
        log_beta, tails, row_sums = {}, {}, {}
        for key, z in scores.items():
            _, mask = jobs[key[0]][key[2]]
            if isinstance(mask, str):
                z = jnp.where(self.strict, z, -jnp.inf)
            elif mask is not None:
                z = z + mask
            sp = jnp.maximum(z, 0.0) + jnp.log(1.0 + jnp.exp(-jnp.abs(z)))
            log_beta[key] = z - sp
            tails[key] = sp.astype(jnp.bfloat16)
            row_sums[key] = (jnp.sum(sp[:, :BLOCK], axis=-1, keepdims=True),
                             jnp.sum(sp[:, BLOCK:], axis=-1, keepdims=True))
        yield
        tails = {key: _dot(sp, self.tail_mat) for key, sp in tails.items()}
        yield
        new_carry, weights = {}, {}
        for u, p in units:
            c = carry[u, p]
            ws = []
            for t in range(len(jobs[u])):
                key = (u, p, t)
                logits = log_beta[key] - tails[key]
                ws += [jnp.exp(logits[:, h * BLOCK:(h + 1) * BLOCK] - c[h]).astype(jnp.bfloat16) for h in range(2)]
                c = tuple(c[h] + row_sums[key][h] for h in range(2))
            new_carry[u, p] = c
            weights[u, p] = jnp.concatenate(ws, axis=1)
        yield
        new_acc = {}
        for u, p in units:
            v_cat = jnp.concatenate([vals[p, id(j)] for j, _ in jobs[u]], axis=0)
            new_acc[u, p] = acc[u, p] + _dot(weights[u, p], v_cat)
        out["carry"], out["acc"] = new_carry, new_acc
        yield

    def fused_stages(self, wave):
        first, subs = self.first, self.subs
        block_at = {d: jnp.maximum(first + d, 0) for d in range(1 - SB_FUSED_BLOCKS, subs)}
        jobs = [[(block_at[u - back],
                  "diag" if back == 0 else (None if u >= back else self._bias_if(first + u - back >= 0)))
                 for back in range(SB_FUSED_BLOCKS)] for u in range(subs)]
        half = len(self.units) // 2
        out = {}
        yield from self._tiles(self.units[wave * half:(wave + 1) * half], jobs, self.carry, self.acc, out)
        self.carry.update(out["carry"])
        self.acc.update(out["acc"])

    def finish(self):
        first, subs, units = self.first, self.subs, self.units

        def alive_after(cs):
            least = functools.reduce(jnp.minimum, [c for key in units for c in cs[key]])
            return (jnp.min(least) < SB_DEAD_SUM).astype(jnp.int32)

        flat = lambda d: tuple(d[key] for key in units)
        unflat = lambda t: dict(zip(units, t))

        def cond(state):
            back, alive, _, _ = state
            return jnp.logical_and(first + subs - 1 - back >= 0, alive > 0)

        def body(state):
            back, _, cs, accs = state
            jobs = [[(jnp.maximum(first + u - back, 0),
                      None if u == subs - 1 else self._bias_if(first + u - back >= 0))] for u in range(subs)]
            out = {}
            for _ in self._tiles(units, jobs, unflat(cs), unflat(accs), out):
                pass
            return back + 1, alive_after(out["carry"]), flat(out["carry"]), flat(out["acc"])

        state = (jnp.int32(SB_FUSED_BLOCKS), alive_after(self.carry), flat(self.carry), flat(self.acc))
        acc = unflat(lax.while_loop(cond, body, state)[3])
        for u, p in units:
            self.out.store(u * BLOCK, p * LANES, acc[u, p])


def _ret_stages(q_ref, k_ref, v_ref, gate_ref, gain_ref, decay_ref, qdec_ref, kdec_ref, sdec_ref,
                out, state_ref):
    n_chunks = q_ref.shape[1] // BLOCK
    low = _low_half_mask((BLOCK, LANES))
    same_head = (lax.broadcasted_iota(jnp.int32, (LANES, LANES), 0) < HEAD_DIM) == _low_half_mask((LANES, LANES))

    def head_mean(t):
        lo_sum = jnp.sum(jnp.where(low, t, 0.0), axis=-1, keepdims=True)
        hi_sum = jnp.sum(jnp.where(low, 0.0, t), axis=-1, keepdims=True)
        return jnp.where(low, lo_sum, hi_sum) * (1.0 / HEAD_DIM)

    tiles = [(p, n) for p in range(RET_HEADS // 2) for n in range(n_chunks)]
    window = lambda ref, p, n: ref[0, n * BLOCK:(n + 1) * BLOCK, p * LANES:(p + 1) * LANES]
    q = {t: window(q_ref, *t) for t in tiles}
    k = {t: window(k_ref, *t) for t in tiles}
    vs = {t: _stack_heads(window(v_ref, *t)) for t in tiles}
    scores = {t: _dot_nt(q[t], _stack_heads(k[t])) for t in tiles}
    kv = {}
    for t in tiles:
        kd = (k[t].astype(jnp.float32) * kdec_ref[t[0]]).astype(jnp.bfloat16)
        kv[t] = lax.dot_general(kd, window(v_ref, *t), (((0,), (0,)), ((), ())),
                                preferred_element_type=jnp.float32)
    yield
    intra = {t: (scores[t] * decay_ref[t[0]]).astype(jnp.bfloat16) for t in tiles}
    yield
    o_intra = {t: _dot(intra[t], vs[t]) for t in tiles}
    yield
    o = {}
    for p in range(RET_HEADS // 2):
        state = state_ref[p]
        for n in range(n_chunks):
            qd = (q[p, n].astype(jnp.float32) * qdec_ref[p]).astype(jnp.bfloat16)
            o[p, n] = o_intra[p, n] + _dot(qd, state.astype(jnp.bfloat16))
            state = sdec_ref[p] * state + jnp.where(same_head, kv[p, n], 0.0)
        state_ref[p] = state
    for t in tiles:
        cen = o[t] - head_mean(o[t])
        var = head_mean(cen * cen)
        normed = cen * lax.rsqrt(var + NORM_EPS) * gain_ref[:, t[0] * LANES:(t[0] + 1) * LANES]
        gate = window(gate_ref, *t)
        out.store(t[1] * BLOCK, t[0] * LANES, gate * jax.nn.sigmoid(gate) * normed)
    yield


class _Columns:
    def __init__(self, ref, col0):
        self.ref, self.col0 = ref, col0

    def store(self, r0, c0, value):
        rows, cols = value.shape
        self.ref[r0:r0 + rows, self.col0 + c0:self.col0 + c0 + cols] = value


FF_CHUNK = 512

STAGE_ROUNDS = {
    "mlp": (1, 2, 3, 4, 5, 6, 7, 8, 9, 10),
    "sb": (1, 2, 3, 4, 5) + (5, 6, 7, 8, 9),
    "swa": (1, 2, 3),
    "ret": (1, 2, 3, 4),
}


def _mlp_stages(mix_ref, x_ref, ga_ref, gb_ref, gpost_ref, gpre_ref, gmlp_ref, wout_ref, wup_ref, wdown_ref,
                o_ref):
    na = _rms(mix_ref[:, 0:SWA_Q_W], ga_ref[...]).astype(jnp.bfloat16)
    nb = _rms(mix_ref[:, SWA_Q_W:SWA_Q_W + SB_W], gb_ref[...]).astype(jnp.bfloat16)
    nc = mix_ref[:, SWA_Q_W + SB_W:].astype(jnp.bfloat16)
    yield
    y = (_dot(na, wout_ref[0:SWA_Q_W, :]) + _dot(nb, wout_ref[SWA_Q_W:SWA_Q_W + SB_W, :])
         + _dot(nc, wout_ref[SWA_Q_W + SB_W:, :]))
    x1 = x_ref[...] + _rms(y, gpost_ref[...])
    hn = _rms(x1, gpre_ref[...]).astype(jnp.bfloat16)
    acc = jnp.zeros_like(x1)
    for f0 in range(0, D_FF, FF_CHUNK):
        u = jnp.maximum(_dot(hn, wup_ref[:, f0:f0 + FF_CHUNK]), 0.0)
        acc = acc + _dot((u * u).astype(jnp.bfloat16), wdown_ref[f0:f0 + FF_CHUNK, :])
        yield
    o_ref[...] = x1 + _rms(acc, gmlp_ref[...])
    yield


def _layer_kernel(sink_ref, qa_ref, ka_ref, kap_ref, va_ref, vap_ref, qb_ref, kb_ref, vb_ref,
                  qc_ref, kc_ref, vc_ref, gate_ref, gain_ref, decay_ref, qdec_ref, kdec_ref, sdec_ref,
                  x_ref, ga_ref, gb_ref, gpost_ref, gpre_ref, gmlp_ref, wout_ref, wup_ref, wdown_ref,
                  o_ref, state_ref, mix_ref, *, seq_tiles, n_tiles):
    s = pl.program_id(0)
    i = jnp.minimum(s, n_tiles - 1) % seq_tiles

    @pl.when(s == 0)
    def _():
        mix_ref[...] = jnp.zeros_like(mix_ref)

    @pl.when(i == 0)
    def _():
        state_ref[...] = jnp.zeros_like(state_ref)

    sb = _StickBreaking(i, qb_ref, kb_ref, vb_ref, _Columns(mix_ref, SWA_Q_W))
    stages = {
        "mlp": _mlp_stages(mix_ref, x_ref, ga_ref, gb_ref, gpost_ref, gpre_ref, gmlp_ref, wout_ref, wup_ref,
                           wdown_ref, o_ref),
        "sb": itertools.chain(sb.fused_stages(0), sb.fused_stages(1)),
        "swa": _swa_stages(i, sink_ref, qa_ref, ka_ref, kap_ref, va_ref, vap_ref, _Columns(mix_ref, 0)),
        "ret": _ret_stages(qc_ref, kc_ref, vc_ref, gate_ref, gain_ref, decay_ref, qdec_ref, kdec_ref, sdec_ref,
                           _Columns(mix_ref, SWA_Q_W + SB_W), state_ref),
    }
    for rnd in range(1, 1 + max(max(r) for r in STAGE_ROUNDS.values())):
        for name, part in stages.items():
            for _ in range(STAGE_ROUNDS[name].count(rnd)):
                next(part)
    for part in stages.values():
        assert next(part, "done") == "done"
    sb.finish()


def _retention_tables():
    h = jnp.arange(RET_HEADS, dtype=jnp.float32)
    log_gamma = jnp.log1p(-(2.0 ** (-5.0 - h)))
    pos = jnp.arange(BLOCK, dtype=jnp.float32)
    rel = pos[:, None] - pos[None, :]
    decay = jnp.where(rel[None] >= 0,
                      jnp.exp(jnp.maximum(rel, 0.0)[None] * log_gamma[:, None, None]), 0.0)
    decay = jnp.swapaxes(decay.reshape(RET_HEADS // 2, 2, BLOCK, BLOCK), 1, 2)
    decay = decay.reshape(RET_HEADS // 2, BLOCK, 2 * BLOCK)
    per_lane = lambda t: jnp.repeat(t.reshape(-1, RET_HEADS // 2, 2), HEAD_DIM, axis=2)
    to_pairs = lambda t: jnp.moveaxis(per_lane(t).reshape(-1, RET_HEADS // 2, LANES), 1, 0)
    q_dec = to_pairs(jnp.exp((pos + 1.0)[:, None] * log_gamma[None, :]))
    k_dec = to_pairs(jnp.exp((BLOCK - 1 - pos)[:, None] * log_gamma[None, :]))
    chunk_decay = to_pairs(jnp.exp(BLOCK * log_gamma)[None, :])
    s_dec = jnp.broadcast_to(jnp.swapaxes(chunk_decay, 1, 2), (RET_HEADS // 2, LANES, LANES))
    return decay, q_dec, k_dec, s_dec


def _mix_mlp(sinks, proj, gate, gn_gain, x2d, ga, gb, g_post, g_pre, g_mlp, layer, w_out, w_up, w_down):
    b, s, _ = proj.shape
    tt = min(TOKEN_TILE, s)
    per = tt // BLOCK
    seq_tiles = s // tt
    n_tiles = b * seq_tiles
    tables = _retention_tables()
    mixer_tile = lambda g: jnp.minimum(g, n_tiles - 1)
    bi = lambda g: mixer_tile(g) // seq_tiles
    ti = lambda g: mixer_tile(g) % seq_tiles
    tile = lambda width, blk: pl.BlockSpec((1, tt, width), lambda g: (bi(g), ti(g), blk * LANES // width))
    prev = lambda blk: pl.BlockSpec((1, BLOCK, LANES), lambda g: (bi(g), jnp.maximum(ti(g) * per - 1, 0), blk))
    whole = lambda width, blk: pl.BlockSpec((1, s, width), lambda g: (bi(g), 0, blk * LANES // width),
                                            pipeline_mode=pl.Buffered(1))
    const = lambda shape: pl.BlockSpec(shape, lambda g: (0,) * len(shape))
    weight = lambda shape: pl.BlockSpec((None,) + shape, lambda g: (layer, 0, 0), pipeline_mode=pl.Buffered(1))
    mlp_rows = pl.BlockSpec((tt, D_MODEL), lambda g: (jnp.maximum(g - 1, 0), 0))
    return pl.pallas_call(
        functools.partial(_layer_kernel, seq_tiles=seq_tiles, n_tiles=n_tiles),
        grid=(n_tiles + 1,),
        in_specs=[
            pl.BlockSpec(memory_space=pltpu.SMEM),
            tile(SWA_Q_W, QA_BLK), tile(LANES, KA_BLK), prev(KA_BLK), tile(LANES, VA_BLK), prev(VA_BLK),
            tile(SB_W, QB_BLK), whole(SB_W, KB_BLK), whole(SB_W, VB_BLK),
            tile(RET_W, QC_BLK), tile(RET_W, KC_BLK), tile(RET_W, VC_BLK), tile(RET_W, 0),
            const((1, RET_W)),
        ] + [const(t.shape) for t in tables] + [
            mlp_rows,
            const((1, SWA_Q_W)), const((1, SB_W)), const((1, D_MODEL)), const((1, D_MODEL)), const((1, D_MODEL)),
            weight((D_MODEL, D_MODEL)), weight((D_MODEL, D_FF)), weight((D_FF, D_MODEL)),
        ],
        out_specs=mlp_rows,
        out_shape=jax.ShapeDtypeStruct(x2d.shape, jnp.float32),
        scratch_shapes=[pltpu.VMEM((RET_HEADS // 2, LANES, LANES), jnp.float32),
                        pltpu.VMEM((tt, D_MODEL), jnp.float32)],
        compiler_params=pltpu.CompilerParams(
            dimension_semantics=("arbitrary",), vmem_limit_bytes=VMEM_LIMIT),
        name="mix_mlp",
    )(sinks, proj, proj, proj, proj, proj, proj, proj, proj, proj, proj, proj, gate, gn_gain, *tables,
      x2d, ga, gb, g_post, g_pre, g_mlp, w_out, w_up, w_down)


def _rope_table(positions):
    inv_freq = ROPE_THETA ** (-jnp.arange(0, HEAD_DIM, 2, dtype=jnp.float32) / HEAD_DIM)
    ang = positions.astype(jnp.float32)[:, None] * inv_freq[None, :]
    cos = jnp.tile(jnp.cos(ang), (1, 4))
    sin = jnp.tile(jnp.concatenate([-jnp.sin(ang), jnp.sin(ang)], axis=1), (1, 2))
    return jnp.stack([cos, sin, cos * QK_SCALE, sin * QK_SCALE])


def _permute_swa_heads(t, axis):
    shape = t.shape
    t = t.reshape(shape[:axis] + (SWA_Q_HEADS, HEAD_DIM) + shape[axis + 1:])
    t = jnp.take(t, jnp.asarray(SWA_HEAD_ORDER), axis=axis)
    return t.reshape(shape)


def _reorder_w_in(w_in):
    edges = np.cumsum([0, SWA_Q_W, SWA_KV_W, SWA_KV_W, SB_W, SB_W, SB_W, RET_W, RET_W, RET_W, RET_W])
    qa, ka, va, qb, kb, vb, qc, kc, vc, gc = [w_in[..., a:e] for a, e in zip(edges[:-1], edges[1:])]
    return jnp.concatenate([_permute_swa_heads(qa, 2), qc, kc, vc, ka, va, qb, kb, vb, gc], axis=-1)


def kernel(x, positions, w_in, w_out, sinks, branch_gain, w_up, w_down,
           norm_mix_pre, norm_mix_post, norm_mlp_pre, norm_mlp_post):
    b, s, d = x.shape
    depth = w_in.shape[0]
    rope_tab = _rope_table(positions)
    w_in = _reorder_w_in(w_in).astype(jnp.bfloat16)
    w_out = jnp.concatenate([_permute_swa_heads(w_out[:, :SWA_Q_W], 1), w_out[:, SWA_Q_W:]],
                            axis=1).astype(jnp.bfloat16)
    gain_a = _permute_swa_heads(branch_gain[:, :SWA_Q_W], 1)
    w_up = w_up.astype(jnp.bfloat16)
    w_down = w_down.astype(jnp.bfloat16)
    row = lambda t: t.reshape(1, -1)

    x2d = x.reshape(b * s, d)
    for l in range(depth):
        proj, gate = _in_proj(x2d, row(norm_mix_pre[l]), w_in, l, rope_tab, s)
        x2d = _mix_mlp(sinks[l], proj.reshape(b, s, PROJ_W), gate.reshape(b, s, RET_W),
                       row(branch_gain[l, SWA_Q_W + SB_W:]), x2d,
                       row(gain_a[l]), row(branch_gain[l, SWA_Q_W:SWA_Q_W + SB_W]),
                       row(norm_mix_post[l]), row(norm_mlp_pre[l]), row(norm_mlp_post[l]),
                       l, w_out, w_up, w_down)
    return x2d.reshape(b, s, d)
```

```python
import functools
import itertools

import numpy as np
import jax
import jax.numpy as jnp
from jax import lax
from jax.experimental import pallas as pl
from jax.experimental.pallas import tpu as pltpu

D_MODEL = 1024
HEAD_DIM = 64
SWA_Q_HEADS = 6
SWA_KV_HEADS = 2
SB_HEADS = 4
RET_HEADS = 6
BLOCK = 128
ROPE_THETA = 10000.0
D_FF = 4 * D_MODEL
NORM_EPS = 1e-6
LANES = 128

SWA_Q_W = SWA_Q_HEADS * HEAD_DIM
SWA_KV_W = SWA_KV_HEADS * HEAD_DIM
SB_W = SB_HEADS * HEAD_DIM
RET_W = RET_HEADS * HEAD_DIM
IN_W = SWA_Q_W + 2 * SWA_KV_W + 3 * SB_W + 4 * RET_W
PROJ_W = IN_W - RET_W

QA_BLK = 0
QC_BLK, KC_BLK, VC_BLK = 3, 6, 9
KA_BLK, VA_BLK = 12, 13
QB_BLK, KB_BLK, VB_BLK = 14, 16, 18
GATE_BLK = 20
QK_SCALE = HEAD_DIM ** -0.5

SWA_HEAD_ORDER = (0, 3, 1, 4, 2, 5)

SB_DEAD_SUM = 104.0
SB_FUSED_BLOCKS = 3

TOKEN_TILE = 512
IN_PROJ_TILE = 1024
VMEM_LIMIT = 62 * 1024 * 1024


def _rms(xf, gain):
    return xf * lax.rsqrt(jnp.mean(xf * xf, axis=-1, keepdims=True) + NORM_EPS) * gain


def _low_half_mask(shape):
    return lax.broadcasted_iota(jnp.int32, shape, len(shape) - 1) < HEAD_DIM


def _stack_heads(pair):
    low = _low_half_mask(pair.shape)
    zero = jnp.zeros_like(pair)
    return jnp.concatenate([jnp.where(low, pair, zero), jnp.where(low, zero, pair)], axis=0)


def _dot_nt(a, b):
    return lax.dot_general(a, b, (((1,), (1,)), ((), ())), preferred_element_type=jnp.float32)


def _dot(a, b):
    return jnp.dot(a, b, preferred_element_type=jnp.float32)


def _in_blk_kind(blk):
    if QA_BLK <= blk < QC_BLK or KC_BLK <= blk < VC_BLK:
        return 1, 1.0
    if QC_BLK <= blk < KC_BLK or blk == KA_BLK:
        return 0, 1.0
    if QB_BLK <= blk < KB_BLK:
        return None, QK_SCALE
    return None, 1.0


def _in_proj_kernel(x_ref, g_ref, w_ref, rope_ref, proj_ref, gate_ref, hn_even_ref, hn_odd_ref):
    s = pl.program_id(0)

    @pl.when(s == 0)
    def _():
        hn_odd_ref[...] = jnp.zeros_like(hn_odd_ref)

    @pl.when(s % 2 == 0)
    def _():
        _in_proj_step(x_ref, g_ref, w_ref, rope_ref, proj_ref, gate_ref, hn_even_ref, hn_odd_ref)

    @pl.when(s % 2 == 1)
    def _():
        _in_proj_step(x_ref, g_ref, w_ref, rope_ref, proj_ref, gate_ref, hn_odd_ref, hn_even_ref)


def _in_proj_step(x_ref, g_ref, w_ref, rope_ref, proj_ref, gate_ref, hn_next_ref, hn_ref):
    hn_next_ref[...] = _rms(x_ref[...], g_ref[...]).astype(jnp.bfloat16)
    first_half = (lax.broadcasted_iota(jnp.int32, (x_ref.shape[0], LANES), 1) % HEAD_DIM) < HEAD_DIM // 2
    chunk = 2 * LANES
    for c0 in range(0, IN_W, chunk):
        width = min(chunk, IN_W - c0)
        p = _dot(hn_ref[...], w_ref[:, c0:c0 + width])
        for s0 in range(0, width, LANES):
            blk = (c0 + s0) // LANES
            pb = p[:, s0:s0 + LANES]
            table, scale = _in_blk_kind(blk)
            if table is not None:
                cos = rope_ref[2 * table]
                sin = rope_ref[2 * table + 1]
                rot = jnp.where(first_half, pltpu.roll(pb, LANES - HEAD_DIM // 2, 1),
                                pltpu.roll(pb, HEAD_DIM // 2, 1))
                pb = pb * cos + rot * sin
            elif scale != 1.0:
                pb = pb * scale
            if blk >= GATE_BLK:
                g0 = (blk - GATE_BLK) * LANES
                gate_ref[:, g0:g0 + LANES] = pb
            else:
                proj_ref[:, blk * LANES:(blk + 1) * LANES] = pb.astype(jnp.bfloat16)


def _in_proj(x2d, gain, w_in, layer, rope_tab, seq):
    n_tok = x2d.shape[0]
    tm = min(IN_PROJ_TILE, seq)
    seq_tiles = seq // tm
    n_tiles = n_tok // tm
    done = lambda i: jnp.maximum(i - 1, 0)
    return pl.pallas_call(
        _in_proj_kernel,
        grid=(n_tiles + 1,),
        in_specs=[
            pl.BlockSpec((tm, D_MODEL), lambda i: (jnp.minimum(i, n_tiles - 1), 0)),
            pl.BlockSpec((1, D_MODEL), lambda i: (0, 0)),
            pl.BlockSpec((None, D_MODEL, IN_W), lambda i: (layer, 0, 0), pipeline_mode=pl.Buffered(1)),
            pl.BlockSpec((4, tm, LANES), lambda i: (0, done(i) % seq_tiles, 0)),
        ],
        out_specs=[
            pl.BlockSpec((tm, PROJ_W), lambda i: (done(i), 0)),
            pl.BlockSpec((tm, RET_W), lambda i: (done(i), 0)),
        ],
        out_shape=[
            jax.ShapeDtypeStruct((n_tok, PROJ_W), jnp.bfloat16),
            jax.ShapeDtypeStruct((n_tok, RET_W), jnp.float32),
        ],
        scratch_shapes=[pltpu.VMEM((tm, D_MODEL), jnp.bfloat16)] * 2,
        compiler_params=pltpu.CompilerParams(
            dimension_semantics=("arbitrary",), vmem_limit_bytes=VMEM_LIMIT),
        name="in_proj",
    )(x2d, gain, w_in, rope_tab)


def _swa_stages(i, sink_ref, q_ref, k_ref, kp_ref, v_ref, vp_ref, gain_ref, out):
    n_blocks = q_ref.shape[1] // BLOCK
    rows = lax.broadcasted_iota(jnp.int32, (BLOCK, 2 * BLOCK), 0)
    cols = lax.broadcasted_iota(jnp.int32, (BLOCK, 2 * BLOCK), 1)
    own = (cols % BLOCK) <= rows
    low_lanes = _low_half_mask((BLOCK, LANES))
    first_bias = jnp.where(i > 0, 0.0, -jnp.inf).astype(jnp.float32)
    ks = [_stack_heads(kp_ref[0])] + [_stack_heads(k_ref[0, n * BLOCK:(n + 1) * BLOCK, :]) for n in range(n_blocks)]
    vs = [_stack_heads(vp_ref[0])] + [_stack_heads(v_ref[0, n * BLOCK:(n + 1) * BLOCK, :]) for n in range(n_blocks)]
    tiles = [(n, c) for n in range(n_blocks) for c in range(SWA_Q_HEADS // 2)]
    scores = {}
    for n, c in tiles:
        q = q_ref[0, n * BLOCK:(n + 1) * BLOCK, c * LANES:(c + 1) * LANES]
        s_prev = _dot_nt(q, ks[n])
        if n == 0:
            s_prev = s_prev + first_bias
        scores[n, c] = jnp.where(own, _dot_nt(q, ks[n + 1]), s_prev)
    yield
    weights = {}
    for n, c in tiles:
        probs, scales = [], []
        for half, head in enumerate(SWA_HEAD_ORDER[2 * c:2 * c + 2]):
            sh = scores[n, c][:, half * BLOCK:(half + 1) * BLOCK]
            sink = sink_ref[head]
            m = jnp.maximum(jnp.max(sh, axis=-1, keepdims=True), sink)
            p = jnp.exp(sh - m)
            probs.append(p)
            scales.append(1.0 / (jnp.sum(p, axis=-1, keepdims=True) + jnp.exp(sink - m)))
        p = jnp.concatenate(probs, axis=1)
        weights[n, c] = (jnp.where(own, p, 0.0).astype(jnp.bfloat16), jnp.where(own, 0.0, p).astype(jnp.bfloat16),
                         jnp.where(low_lanes, scales[0], scales[1]))
    yield
    for n in range(n_blocks):
        heads = []
        for c in range(SWA_Q_HEADS // 2):
            p_own, p_prev, scale = weights[n, c]
            heads.append((_dot(p_own, vs[n + 1]) + _dot(p_prev, vs[n])) * scale)
        out.store(n * BLOCK, 0, _rms(jnp.concatenate(heads, axis=1), gain_ref[...]))
    yield


class _StickBreaking:
    def __init__(self, i, q_ref, k_ref, v_ref, gain_ref, out):
        self.k_ref, self.v_ref, self.gain_ref, self.out = k_ref, v_ref, gain_ref, out
        self.subs = q_ref.shape[1] // BLOCK
        self.first = i * self.subs
        pairs = SB_HEADS // 2
        wide = 2 * BLOCK
        self.units = [(u, p) for u in range(self.subs) for p in range(pairs)]
        self.q = {(u, p): q_ref[0, u * BLOCK:(u + 1) * BLOCK, p * LANES:(p + 1) * LANES] for u, p in self.units}
        rows = lax.broadcasted_iota(jnp.int32, (BLOCK, wide), 0)
        cols = lax.broadcasted_iota(jnp.int32, (BLOCK, wide), 1)
        self.strict = (cols % BLOCK) < rows
        kj = lax.broadcasted_iota(jnp.int32, (wide, wide), 0)
        ks = lax.broadcasted_iota(jnp.int32, (wide, wide), 1)
        self.tail_mat = jnp.where(((kj < BLOCK) == (ks < BLOCK)) & (kj > ks), 1.0, 0.0).astype(jnp.bfloat16)
        self.carry = {key: (jnp.zeros((BLOCK, 1), jnp.float32),) * 2 for key in self.units}
        self.acc = {key: jnp.zeros((BLOCK, LANES), jnp.float32) for key in self.units}

    def _load(self, ref, p, j):
        start = pl.multiple_of(j * BLOCK, BLOCK)
        return _stack_heads(ref[0, pl.ds(start, BLOCK), p * LANES:(p + 1) * LANES])

    @staticmethod
    def _bias_if(valid):
        return jnp.where(valid, 0.0, -jnp.inf).astype(jnp.float32)

    def _tiles(self, units, jobs, carry, acc, out):
        keys, vals = {}, {}
        for u, p in units:
            for j, _ in jobs[u]:
                if (p, id(j)) not in keys:
                    keys[p, id(j)], vals[p, id(j)] = self._load(self.k_ref, p, j), self._load(self.v_ref, p, j)
        scores = {(u, p, t): _dot_nt(self.q[u, p], keys[p, id(j)])
                  for u, p in units for t, (j, _) in enumerate(jobs[u])}
        yield
        log_beta, tails, row_sums = {}, {}, {}
        for key, z in scores.items():
            _, mask = jobs[key[0]][key[2]]
            if isinstance(mask, str):
                z = jnp.where(self.strict, z, -jnp.inf)
            elif mask is not None:
                z = z + mask
            sp = jnp.maximum(z, 0.0) + jnp.log(1.0 + jnp.exp(-jnp.abs(z)))
            log_beta[key] = z - sp
            tails[key] = sp.astype(jnp.bfloat16)
            row_sums[key] = (jnp.sum(sp[:, :BLOCK], axis=-1, keepdims=True),
                             jnp.sum(sp[:, BLOCK:], axis=-1, keepdims=True))
        yield
        tails = {key: _dot(sp, self.tail_mat) for key, sp in tails.items()}
        yield
        new_carry, weights = {}, {}
        for u, p in units:
            c = carry[u, p]
            ws = []
            for t in range(len(jobs[u])):
                key = (u, p, t)
                logits = log_beta[key] - tails[key]
                ws += [jnp.exp(logits[:, h * BLOCK:(h + 1) * BLOCK] - c[h]).astype(jnp.bfloat16) for h in range(2)]
                c = tuple(c[h] + row_sums[key][h] for h in range(2))
            new_carry[u, p] = c
            weights[u, p] = jnp.concatenate(ws, axis=1)
        yield
        new_acc = {}
        for u, p in units:
            v_cat = jnp.concatenate([vals[p, id(j)] for j, _ in jobs[u]], axis=0)
            new_acc[u, p] = acc[u, p] + _dot(weights[u, p], v_cat)
        out["carry"], out["acc"] = new_carry, new_acc
        yield

    def fused_stages(self, wave):
        first, subs = self.first, self.subs
        block_at = {d: jnp.maximum(first + d, 0) for d in range(1 - SB_FUSED_BLOCKS, subs)}
        jobs = [[(block_at[u - back],
                  "diag" if back == 0 else (None if u >= back else self._bias_if(first + u - back >= 0)))
                 for back in range(SB_FUSED_BLOCKS)] for u in range(subs)]
        half = len(self.units) // 2
        out = {}
        yield from self._tiles(self.units[wave * half:(wave + 1) * half], jobs, self.carry, self.acc, out)
        self.carry.update(out["carry"])
        self.acc.update(out["acc"])

    def finish(self):
        first, subs, units = self.first, self.subs, self.units

        def alive_after(cs):
            least = functools.reduce(jnp.minimum, [c for key in units for c in cs[key]])
            return (jnp.min(least) < SB_DEAD_SUM).astype(jnp.int32)

        flat = lambda d: tuple(d[key] for key in units)
        unflat = lambda t: dict(zip(units, t))

        def cond(state):
            back, alive, _, _ = state
            return jnp.logical_and(first + subs - 1 - back >= 0, alive > 0)

        def body(state):
            back, _, cs, accs = state
            jobs = [[(jnp.maximum(first + u - back, 0),
                      None if u == subs - 1 else self._bias_if(first + u - back >= 0))] for u in range(subs)]
            out = {}
            for _ in self._tiles(units, jobs, unflat(cs), unflat(accs), out):
                pass
            return back + 1, alive_after(out["carry"]), flat(out["carry"]), flat(out["acc"])

        state = (jnp.int32(SB_FUSED_BLOCKS), alive_after(self.carry), flat(self.carry), flat(self.acc))
        acc = unflat(lax.while_loop(cond, body, state)[3])
        for u in range(subs):
            heads = jnp.concatenate([acc[u, p] for p in range(SB_HEADS // 2)], axis=1)
            self.out.store(u * BLOCK, 0, _rms(heads, self.gain_ref[...]))


def _ret_stages(q_ref, k_ref, v_ref, gate_ref, gain_ref, decay_ref, qdec_ref, kdec_ref, sdec_ref,
                out, state_ref):
    n_chunks = q_ref.shape[1] // BLOCK
    low = _low_half_mask((BLOCK, LANES))
    same_head = (lax.broadcasted_iota(jnp.int32, (LANES, LANES), 0) < HEAD_DIM) == _low_half_mask((LANES, LANES))

    def head_mean(t):
        lo_sum = jnp.sum(jnp.where(low, t, 0.0), axis=-1, keepdims=True)
        hi_sum = jnp.sum(jnp.where(low, 0.0, t), axis=-1, keepdims=True)
        return jnp.where(low, lo_sum, hi_sum) * (1.0 / HEAD_DIM)

    tiles = [(p, n) for p in range(RET_HEADS // 2) for n in range(n_chunks)]
    window = lambda ref, p, n: ref[0, n * BLOCK:(n + 1) * BLOCK, p * LANES:(p + 1) * LANES]
    q = {t: window(q_ref, *t) for t in tiles}
    k = {t: window(k_ref, *t) for t in tiles}
    vs = {t: _stack_heads(window(v_ref, *t)) for t in tiles}
    scores = {t: _dot_nt(q[t], _stack_heads(k[t])) for t in tiles}
    kv = {}
    for t in tiles:
        kd = (k[t].astype(jnp.float32) * kdec_ref[t[0]]).astype(jnp.bfloat16)
        kv[t] = lax.dot_general(kd, window(v_ref, *t), (((0,), (0,)), ((), ())),
                                preferred_element_type=jnp.float32)
    yield
    intra = {t: (scores[t] * decay_ref[t[0]]).astype(jnp.bfloat16) for t in tiles}
    yield
    o_intra = {t: _dot(intra[t], vs[t]) for t in tiles}
    yield
    o = {}
    for p in range(RET_HEADS // 2):
        state = state_ref[p]
        for n in range(n_chunks):
            qd = (q[p, n].astype(jnp.float32) * qdec_ref[p]).astype(jnp.bfloat16)
            o[p, n] = o_intra[p, n] + _dot(qd, state.astype(jnp.bfloat16))
            state = sdec_ref[p] * state + jnp.where(same_head, kv[p, n], 0.0)
        state_ref[p] = state
    for t in tiles:
        cen = o[t] - head_mean(o[t])
        var = head_mean(cen * cen)
        normed = cen * lax.rsqrt(var + NORM_EPS) * gain_ref[:, t[0] * LANES:(t[0] + 1) * LANES]
        gate = window(gate_ref, *t)
        out.store(t[1] * BLOCK, t[0] * LANES, gate * jax.nn.sigmoid(gate) * normed)
    yield


class _Columns:
    def __init__(self, ref, col0):
        self.ref, self.col0 = ref, col0

    def store(self, r0, c0, value):
        rows, cols = value.shape
        self.ref[r0:r0 + rows, self.col0 + c0:self.col0 + c0 + cols] = value.astype(self.ref.dtype)


FF_CHUNK = 512

STAGE_ROUNDS = {
    "mlp": (1, 2, 3, 4, 5, 6, 7, 8, 9, 10),
    "sb": (1, 2, 3, 4, 5) + (5, 6, 7, 8, 9),
    "swa": (1, 2, 3),
    "ret": (1, 2, 3, 4),
}


def _mlp_stages(mix_ref, x_ref, gpost_ref, gpre_ref, gmlp_ref, wout_ref, wup_ref, wdown_ref, o_ref):
    mixed = mix_ref[...]
    yield
    y = _dot(mixed, wout_ref[...])
    x1 = x_ref[...] + _rms(y, gpost_ref[...])
    hn = _rms(x1, gpre_ref[...]).astype(jnp.bfloat16)
    acc = jnp.zeros_like(x1)
    for f0 in range(0, D_FF, FF_CHUNK):
        u = jnp.maximum(_dot(hn, wup_ref[:, f0:f0 + FF_CHUNK]), 0.0)
        acc = acc + _dot((u * u).astype(jnp.bfloat16), wdown_ref[f0:f0 + FF_CHUNK, :])
        yield
    o_ref[...] = x1 + _rms(acc, gmlp_ref[...])
    yield


def _layer_kernel(sink_ref, qa_ref, ka_ref, kap_ref, va_ref, vap_ref, qb_ref, kb_ref, vb_ref,
                  qc_ref, kc_ref, vc_ref, gate_ref, gain_ref, decay_ref, qdec_ref, kdec_ref, sdec_ref,
                  x_ref, ga_ref, gb_ref, gpost_ref, gpre_ref, gmlp_ref, wout_ref, wup_ref, wdown_ref,
                  o_ref, state_ref, mix_ref, *, seq_tiles, n_tiles):
    s = pl.program_id(0)
    i = jnp.minimum(s, n_tiles - 1) % seq_tiles

    @pl.when(s == 0)
    def _():
        mix_ref[...] = jnp.zeros_like(mix_ref)

    @pl.when(i == 0)
    def _():
        state_ref[...] = jnp.zeros_like(state_ref)

    sb = _StickBreaking(i, qb_ref, kb_ref, vb_ref, gb_ref, _Columns(mix_ref, SWA_Q_W))
    stages = {
        "mlp": _mlp_stages(mix_ref, x_ref, gpost_ref, gpre_ref, gmlp_ref, wout_ref, wup_ref, wdown_ref, o_ref),
        "sb": itertools.chain(sb.fused_stages(0), sb.fused_stages(1)),
        "swa": _swa_stages(i, sink_ref, qa_ref, ka_ref, kap_ref, va_ref, vap_ref, ga_ref, _Columns(mix_ref, 0)),
        "ret": _ret_stages(qc_ref, kc_ref, vc_ref, gate_ref, gain_ref, decay_ref, qdec_ref, kdec_ref, sdec_ref,
                           _Columns(mix_ref, SWA_Q_W + SB_W), state_ref),
    }
    for rnd in range(1, 1 + max(max(r) for r in STAGE_ROUNDS.values())):
        for name, part in stages.items():
            for _ in range(STAGE_ROUNDS[name].count(rnd)):
                next(part)
    for part in stages.values():
        assert next(part, "done") == "done"
    sb.finish()


def _retention_tables():
    h = jnp.arange(RET_HEADS, dtype=jnp.float32)
    log_gamma = jnp.log1p(-(2.0 ** (-5.0 - h)))
    pos = jnp.arange(BLOCK, dtype=jnp.float32)
    rel = pos[:, None] - pos[None, :]
    decay = jnp.where(rel[None] >= 0,
                      jnp.exp(jnp.maximum(rel, 0.0)[None] * log_gamma[:, None, None]), 0.0)
    decay = jnp.swapaxes(decay.reshape(RET_HEADS // 2, 2, BLOCK, BLOCK), 1, 2)
    decay = decay.reshape(RET_HEADS // 2, BLOCK, 2 * BLOCK)
    per_lane = lambda t: jnp.repeat(t.reshape(-1, RET_HEADS // 2, 2), HEAD_DIM, axis=2)
    to_pairs = lambda t: jnp.moveaxis(per_lane(t).reshape(-1, RET_HEADS // 2, LANES), 1, 0)
    q_dec = to_pairs(jnp.exp((pos + 1.0)[:, None] * log_gamma[None, :]))
    k_dec = to_pairs(jnp.exp((BLOCK - 1 - pos)[:, None] * log_gamma[None, :]))
    chunk_decay = to_pairs(jnp.exp(BLOCK * log_gamma)[None, :])
    s_dec = jnp.broadcast_to(jnp.swapaxes(chunk_decay, 1, 2), (RET_HEADS // 2, LANES, LANES))
    return decay, q_dec, k_dec, s_dec


def _mix_mlp(sinks, proj, gate, gn_gain, x2d, ga, gb, g_post, g_pre, g_mlp, layer, w_out, w_up, w_down):
    b, s, _ = proj.shape
    tt = min(TOKEN_TILE, s)
    per = tt // BLOCK
    seq_tiles = s // tt
    n_tiles = b * seq_tiles
    tables = _retention_tables()
    mixer_tile = lambda g: jnp.minimum(g, n_tiles - 1)
    bi = lambda g: mixer_tile(g) // seq_tiles
    ti = lambda g: mixer_tile(g) % seq_tiles
    tile = lambda width, blk: pl.BlockSpec((1, tt, width), lambda g: (bi(g), ti(g), blk * LANES // width))
    prev = lambda blk: pl.BlockSpec((1, BLOCK, LANES), lambda g: (bi(g), jnp.maximum(ti(g) * per - 1, 0), blk))
    whole = lambda width, blk: pl.BlockSpec((1, s, width), lambda g: (bi(g), 0, blk * LANES // width),
                                            pipeline_mode=pl.Buffered(1))
    const = lambda shape: pl.BlockSpec(shape, lambda g: (0,) * len(shape))
    weight = lambda shape: pl.BlockSpec((None,) + shape, lambda g: (layer, 0, 0), pipeline_mode=pl.Buffered(1))
    mlp_rows = pl.BlockSpec((tt, D_MODEL), lambda g: (jnp.maximum(g - 1, 0), 0))
    return pl.pallas_call(
        functools.partial(_layer_kernel, seq_tiles=seq_tiles, n_tiles=n_tiles),
        grid=(n_tiles + 1,),
        in_specs=[
            pl.BlockSpec(memory_space=pltpu.SMEM),
            tile(SWA_Q_W, QA_BLK), tile(LANES, KA_BLK), prev(KA_BLK), tile(LANES, VA_BLK), prev(VA_BLK),
            tile(SB_W, QB_BLK), whole(SB_W, KB_BLK), whole(SB_W, VB_BLK),
            tile(RET_W, QC_BLK), tile(RET_W, KC_BLK), tile(RET_W, VC_BLK), tile(RET_W, 0),
            const((1, RET_W)),
        ] + [const(t.shape) for t in tables] + [
            mlp_rows,
            const((1, SWA_Q_W)), const((1, SB_W)), const((1, D_MODEL)), const((1, D_MODEL)), const((1, D_MODEL)),
            weight((D_MODEL, D_MODEL)), weight((D_MODEL, D_FF)), weight((D_FF, D_MODEL)),
        ],
        out_specs=mlp_rows,
        out_shape=jax.ShapeDtypeStruct(x2d.shape, jnp.float32),
        scratch_shapes=[pltpu.VMEM((RET_HEADS // 2, LANES, LANES), jnp.float32),
                        pltpu.VMEM((tt, D_MODEL), jnp.bfloat16)],
        compiler_params=pltpu.CompilerParams(
            dimension_semantics=("arbitrary",), vmem_limit_bytes=VMEM_LIMIT),
        name="mix_mlp",
    )(sinks, proj, proj, proj, proj, proj, proj, proj, proj, proj, proj, proj, gate, gn_gain, *tables,
      x2d, ga, gb, g_post, g_pre, g_mlp, w_out, w_up, w_down)


def _rope_table(positions):
    inv_freq = ROPE_THETA ** (-jnp.arange(0, HEAD_DIM, 2, dtype=jnp.float32) / HEAD_DIM)
    ang = positions.astype(jnp.float32)[:, None] * inv_freq[None, :]
    cos = jnp.tile(jnp.cos(ang), (1, 4))
    sin = jnp.tile(jnp.concatenate([-jnp.sin(ang), jnp.sin(ang)], axis=1), (1, 2))
    return jnp.stack([cos, sin, cos * QK_SCALE, sin * QK_SCALE])


def _permute_swa_heads(t, axis):
    shape = t.shape
    t = t.reshape(shape[:axis] + (SWA_Q_HEADS, HEAD_DIM) + shape[axis + 1:])
    t = jnp.take(t, jnp.asarray(SWA_HEAD_ORDER), axis=axis)
    return t.reshape(shape)


def _reorder_w_in(w_in):
    edges = np.cumsum([0, SWA_Q_W, SWA_KV_W, SWA_KV_W, SB_W, SB_W, SB_W, RET_W, RET_W, RET_W, RET_W])
    qa, ka, va, qb, kb, vb, qc, kc, vc, gc = [w_in[..., a:e] for a, e in zip(edges[:-1], edges[1:])]
    return jnp.concatenate([_permute_swa_heads(qa, 2), qc, kc, vc, ka, va, qb, kb, vb, gc], axis=-1)


def kernel(x, positions, w_in, w_out, sinks, branch_gain, w_up, w_down,
           norm_mix_pre, norm_mix_post, norm_mlp_pre, norm_mlp_post):
    b, s, d = x.shape
    depth = w_in.shape[0]
    rope_tab = _rope_table(positions)
    w_in = _reorder_w_in(w_in).astype(jnp.bfloat16)
    w_out = jnp.concatenate([_permute_swa_heads(w_out[:, :SWA_Q_W], 1), w_out[:, SWA_Q_W:]],
                            axis=1).astype(jnp.bfloat16)
    gain_a = _permute_swa_heads(branch_gain[:, :SWA_Q_W], 1)
    w_up = w_up.astype(jnp.bfloat16)
    w_down = w_down.astype(jnp.bfloat16)
    row = lambda t: t.reshape(1, -1)

    x2d = x.reshape(b * s, d)
    for l in range(depth):
        proj, gate = _in_proj(x2d, row(norm_mix_pre[l]), w_in, l, rope_tab, s)
        x2d = _mix_mlp(sinks[l], proj.reshape(b, s, PROJ_W), gate.reshape(b, s, RET_W),
                       row(branch_gain[l, SWA_Q_W + SB_W:]), x2d,
                       row(gain_a[l]), row(branch_gain[l, SWA_Q_W:SWA_Q_W + SB_W]),
                       row(norm_mix_post[l]), row(norm_mlp_pre[l]), row(norm_mlp_post[l]),
                       l, w_out, w_up, w_down)
    return x2d.reshape(b, s, d)
```

```python
import functools
import itertools

import numpy as np
import jax
import jax.numpy as jnp
from jax import lax
from jax.experimental import pallas as pl
from jax.experimental.pallas import tpu as pltpu

D_MODEL = 1024
HEAD_DIM = 64
SWA_Q_HEADS = 6
SWA_KV_HEADS = 2
SB_HEADS = 4
RET_HEADS = 6
BLOCK = 128
ROPE_THETA = 10000.0
D_FF = 4 * D_MODEL
NORM_EPS = 1e-6
LANES = 128

SWA_Q_W = SWA_Q_HEADS * HEAD_DIM
SWA_KV_W = SWA_KV_HEADS * HEAD_DIM
SB_W = SB_HEADS * HEAD_DIM
RET_W = RET_HEADS * HEAD_DIM
IN_W = SWA_Q_W + 2 * SWA_KV_W + 3 * SB_W + 4 * RET_W
PROJ_W = IN_W - RET_W

QA_BLK = 0
QC_BLK, KC_BLK, VC_BLK = 3, 6, 9
KA_BLK, VA_BLK = 12, 13
QB_BLK, KB_BLK, VB_BLK = 14, 16, 18
GATE_BLK = 20
QK_SCALE = HEAD_DIM ** -0.5

SWA_HEAD_ORDER = (0, 3, 1, 4, 2, 5)

SB_DEAD_SUM = 104.0
SB_FUSED_BLOCKS = 3

TOKEN_TILE = 512
VMEM_LIMIT = 62 * 1024 * 1024


def _rms(xf, gain):
    return xf * lax.rsqrt(jnp.mean(xf * xf, axis=-1, keepdims=True) + NORM_EPS) * gain


def _low_half_mask(shape):
    return lax.broadcasted_iota(jnp.int32, shape, len(shape) - 1) < HEAD_DIM


def _stack_heads(pair):
    low = _low_half_mask(pair.shape)
    zero = jnp.zeros_like(pair)
    return jnp.concatenate([jnp.where(low, pair, zero), jnp.where(low, zero, pair)], axis=0)


def _dot_nt(a, b):
    return lax.dot_general(a, b, (((1,), (1,)), ((), ())), preferred_element_type=jnp.float32)


def _dot(a, b):
    return jnp.dot(a, b, preferred_element_type=jnp.float32)


def _in_blk_kind(blk):
    if QA_BLK <= blk < QC_BLK or KC_BLK <= blk < VC_BLK:
        return 1, 1.0
    if QC_BLK <= blk < KC_BLK or blk == KA_BLK:
        return 0, 1.0
    if QB_BLK <= blk < KB_BLK:
        return None, QK_SCALE
    return None, 1.0


def _in_proj_kernel(x_ref, g_ref, w_ref, rope_ref, proj_ref, gate_ref):
    hn = _rms(x_ref[...], g_ref[...]).astype(jnp.bfloat16)
    first_half = (lax.broadcasted_iota(jnp.int32, (x_ref.shape[0], LANES), 1) % HEAD_DIM) < HEAD_DIM // 2
    chunk = 2 * LANES
    for c0 in range(0, IN_W, chunk):
        width = min(chunk, IN_W - c0)
        p = _dot(hn, w_ref[:, c0:c0 + width])
        for s0 in range(0, width, LANES):
            blk = (c0 + s0) // LANES
            pb = p[:, s0:s0 + LANES]
            table, scale = _in_blk_kind(blk)
            if table is not None:
                cos = rope_ref[2 * table]
                sin = rope_ref[2 * table + 1]
                rot = jnp.where(first_half, pltpu.roll(pb, LANES - HEAD_DIM // 2, 1),
                                pltpu.roll(pb, HEAD_DIM // 2, 1))
                pb = pb * cos + rot * sin
            elif scale != 1.0:
                pb = pb * scale
            if blk >= GATE_BLK:
                g0 = (blk - GATE_BLK) * LANES
                gate_ref[:, g0:g0 + LANES] = pb
            else:
                proj_ref[:, blk * LANES:(blk + 1) * LANES] = pb.astype(jnp.bfloat16)


def _in_proj(x2d, gain, w_in, layer, rope_tab, seq):
    n_tok = x2d.shape[0]
    tm = min(TOKEN_TILE, seq)
    seq_tiles = seq // tm
    return pl.pallas_call(
        _in_proj_kernel,
        grid=(n_tok // tm,),
        in_specs=[
            pl.BlockSpec((tm, D_MODEL), lambda i: (i, 0)),
            pl.BlockSpec((1, D_MODEL), lambda i: (0, 0)),
            pl.BlockSpec((None, D_MODEL, IN_W), lambda i: (layer, 0, 0), pipeline_mode=pl.Buffered(1)),
            pl.BlockSpec((4, tm, LANES), lambda i: (0, i % seq_tiles, 0)),
        ],
        out_specs=[
            pl.BlockSpec((tm, PROJ_W), lambda i: (i, 0)),
            pl.BlockSpec((tm, RET_W), lambda i: (i, 0)),
        ],
        out_shape=[
            jax.ShapeDtypeStruct((n_tok, PROJ_W), jnp.bfloat16),
            jax.ShapeDtypeStruct((n_tok, RET_W), jnp.float32),
        ],
        compiler_params=pltpu.CompilerParams(
            dimension_semantics=("arbitrary",), vmem_limit_bytes=VMEM_LIMIT),
        name="in_proj",
    )(x2d, gain, w_in, rope_tab)


def _swa_stages(i, sink_ref, q_ref, k_ref, kp_ref, v_ref, vp_ref, gain_ref, out):
    n_blocks = q_ref.shape[1] // BLOCK
    rows = lax.broadcasted_iota(jnp.int32, (BLOCK, 2 * BLOCK), 0)
    cols = lax.broadcasted_iota(jnp.int32, (BLOCK, 2 * BLOCK), 1)
    own = (cols % BLOCK) <= rows
    low_lanes = _low_half_mask((BLOCK, LANES))
    first_bias = jnp.where(i > 0, 0.0, -jnp.inf).astype(jnp.float32)
    ks = [_stack_heads(kp_ref[0])] + [_stack_heads(k_ref[0, n * BLOCK:(n + 1) * BLOCK, :]) for n in range(n_blocks)]
    vs = [_stack_heads(vp_ref[0])] + [_stack_heads(v_ref[0, n * BLOCK:(n + 1) * BLOCK, :]) for n in range(n_blocks)]
    tiles = [(n, c) for n in range(n_blocks) for c in range(SWA_Q_HEADS // 2)]
    scores = {}
    for n, c in tiles:
        q = q_ref[0, n * BLOCK:(n + 1) * BLOCK, c * LANES:(c + 1) * LANES]
        s_prev = _dot_nt(q, ks[n])
        if n == 0:
            s_prev = s_prev + first_bias
        scores[n, c] = jnp.where(own, _dot_nt(q, ks[n + 1]), s_prev)
    yield
    weights = {}
    for n, c in tiles:
        probs, scales = [], []
        for half, head in enumerate(SWA_HEAD_ORDER[2 * c:2 * c + 2]):
            sh = scores[n, c][:, half * BLOCK:(half + 1) * BLOCK]
            sink = sink_ref[head]
            m = jnp.maximum(jnp.max(sh, axis=-1, keepdims=True), sink)
            p = jnp.exp(sh - m)
            probs.append(p)
            scales.append(1.0 / (jnp.sum(p, axis=-1, keepdims=True) + jnp.exp(sink - m)))
        p = jnp.concatenate(probs, axis=1)
        weights[n, c] = (jnp.where(own, p, 0.0).astype(jnp.bfloat16), jnp.where(own, 0.0, p).astype(jnp.bfloat16),
                         jnp.where(low_lanes, scales[0], scales[1]))
    yield
    for n in range(n_blocks):
        heads = []
        for c in range(SWA_Q_HEADS // 2):
            p_own, p_prev, scale = weights[n, c]
            heads.append((_dot(p_own, vs[n + 1]) + _dot(p_prev, vs[n])) * scale)
        out.store(n * BLOCK, 0, _rms(jnp.concatenate(heads, axis=1), gain_ref[...]))
    yield


class _StickBreaking:
    def __init__(self, i, q_ref, k_ref, v_ref, gain_ref, out):
        self.k_ref, self.v_ref, self.gain_ref, self.out = k_ref, v_ref, gain_ref, out
        self.subs = q_ref.shape[1] // BLOCK
        self.first = i * self.subs
        pairs = SB_HEADS // 2
        wide = 2 * BLOCK
        self.units = [(u, p) for u in range(self.subs) for p in range(pairs)]
        self.q = {(u, p): q_ref[0, u * BLOCK:(u + 1) * BLOCK, p * LANES:(p + 1) * LANES] for u, p in self.units}
        rows = lax.broadcasted_iota(jnp.int32, (BLOCK, wide), 0)
        cols = lax.broadcasted_iota(jnp.int32, (BLOCK, wide), 1)
        self.strict = (cols % BLOCK) < rows
        kj = lax.broadcasted_iota(jnp.int32, (wide, wide), 0)
        ks = lax.broadcasted_iota(jnp.int32, (wide, wide), 1)
        self.tail_mat = jnp.where(((kj < BLOCK) == (ks < BLOCK)) & (kj > ks), 1.0, 0.0).astype(jnp.bfloat16)
        self.carry = {key: (jnp.zeros((BLOCK, 1), jnp.float32),) * 2 for key in self.units}
        self.acc = {key: jnp.zeros((BLOCK, LANES), jnp.float32) for key in self.units}

    def _load(self, ref, p, j):
        start = pl.multiple_of(j * BLOCK, BLOCK)
        return _stack_heads(ref[pl.ds(start, BLOCK), p * LANES:(p + 1) * LANES])

    @staticmethod
    def _bias_if(valid):
        return jnp.where(valid, 0.0, -jnp.inf).astype(jnp.float32)

    def _tiles(self, units, jobs, carry, acc, out):
        keys, vals = {}, {}
        for u, p in units:
            for j, _ in jobs[u]:
                if (p, id(j)) not in keys:
                    keys[p, id(j)], vals[p, id(j)] = self._load(self.k_ref, p, j), self._load(self.v_ref, p, j)
        scores = {(u, p, t): _dot_nt(self.q[u, p], keys[p, id(j)])
                  for u, p in units for t, (j, _) in enumerate(jobs[u])}
        yield
        log_beta, tails, row_sums = {}, {}, {}
        for key, z in scores.items():
            _, mask = jobs[key[0]][key[2]]
            if isinstance(mask, str):
                z = jnp.where(self.strict, z, -jnp.inf)
            elif mask is not None:
                z = z + mask
            sp = jnp.maximum(z, 0.0) + jnp.log(1.0 + jnp.exp(-jnp.abs(z)))
            log_beta[key] = z - sp
            tails[key] = sp.astype(jnp.bfloat16)
            row_sums[key] = (jnp.sum(sp[:, :BLOCK], axis=-1, keepdims=True),
                             jnp.sum(sp[:, BLOCK:], axis=-1, keepdims=True))
        yield
        tails = {key: _dot(sp, self.tail_mat) for key, sp in tails.items()}
        yield
        new_carry, weights = {}, {}
        for u, p in units:
            c = carry[u, p]
            ws = []
            for t in range(len(jobs[u])):
                key = (u, p, t)
                logits = log_beta[key] - tails[key]
                ws += [jnp.exp(logits[:, h * BLOCK:(h + 1) * BLOCK] - c[h]).astype(jnp.bfloat16) for h in range(2)]
                c = tuple(c[h] + row_sums[key][h] for h in range(2))
            new_carry[u, p] = c
            weights[u, p] = jnp.concatenate(ws, axis=1)
        yield
        new_acc = {}
        for u, p in units:
            v_cat = jnp.concatenate([vals[p, id(j)] for j, _ in jobs[u]], axis=0)
            new_acc[u, p] = acc[u, p] + _dot(weights[u, p], v_cat)
        out["carry"], out["acc"] = new_carry, new_acc
        yield

    def fused_stages(self, wave):
        first, subs = self.first, self.subs
        block_at = {d: jnp.maximum(first + d, 0) for d in range(1 - SB_FUSED_BLOCKS, subs)}
        jobs = [[(block_at[u - back],
                  "diag" if back == 0 else (None if u >= back else self._bias_if(first + u - back >= 0)))
                 for back in range(SB_FUSED_BLOCKS)] for u in range(subs)]
        half = len(self.units) // 2
        out = {}
        yield from self._tiles(self.units[wave * half:(wave + 1) * half], jobs, self.carry, self.acc, out)
        self.carry.update(out["carry"])
        self.acc.update(out["acc"])

    def finish(self):
        first, subs, units = self.first, self.subs, self.units

        def alive_after(cs):
            least = functools.reduce(jnp.minimum, [c for key in units for c in cs[key]])
            return (jnp.min(least) < SB_DEAD_SUM).astype(jnp.int32)

        flat = lambda d: tuple(d[key] for key in units)
        unflat = lambda t: dict(zip(units, t))

        def cond(state):
            back, alive, _, _ = state
            return jnp.logical_and(first + subs - 1 - back >= 0, alive > 0)

        def body(state):
            back, _, cs, accs = state
            jobs = [[(jnp.maximum(first + u - back, 0),
                      None if u == subs - 1 else self._bias_if(first + u - back >= 0))] for u in range(subs)]
            out = {}
            for _ in self._tiles(units, jobs, unflat(cs), unflat(accs), out):
                pass
            return back + 1, alive_after(out["carry"]), flat(out["carry"]), flat(out["acc"])

        state = (jnp.int32(SB_FUSED_BLOCKS), alive_after(self.carry), flat(self.carry), flat(self.acc))
        acc = unflat(lax.while_loop(cond, body, state)[3])
        for u in range(subs):
            heads = jnp.concatenate([acc[u, p] for p in range(SB_HEADS // 2)], axis=1)
            self.out.store(u * BLOCK, 0, _rms(heads, self.gain_ref[...]))


def _ret_stages(q_ref, k_ref, v_ref, gate_ref, gain_ref, decay_ref, qdec_ref, kdec_ref, sdec_ref,
                out, state_ref):
    n_chunks = q_ref.shape[1] // BLOCK
    low = _low_half_mask((BLOCK, LANES))
    same_head = (lax.broadcasted_iota(jnp.int32, (LANES, LANES), 0) < HEAD_DIM) == _low_half_mask((LANES, LANES))

    def head_mean(t):
        lo_sum = jnp.sum(jnp.where(low, t, 0.0), axis=-1, keepdims=True)
        hi_sum = jnp.sum(jnp.where(low, 0.0, t), axis=-1, keepdims=True)
        return jnp.where(low, lo_sum, hi_sum) * (1.0 / HEAD_DIM)

    tiles = [(p, n) for p in range(RET_HEADS // 2) for n in range(n_chunks)]
    window = lambda ref, p, n: ref[0, n * BLOCK:(n + 1) * BLOCK, p * LANES:(p + 1) * LANES]
    q = {t: window(q_ref, *t) for t in tiles}
    k = {t: window(k_ref, *t) for t in tiles}
    vs = {t: _stack_heads(window(v_ref, *t)) for t in tiles}
    scores = {t: _dot_nt(q[t], _stack_heads(k[t])) for t in tiles}
    kv = {}
    for t in tiles:
        kd = (k[t].astype(jnp.float32) * kdec_ref[t[0]]).astype(jnp.bfloat16)
        kv[t] = lax.dot_general(kd, window(v_ref, *t), (((0,), (0,)), ((), ())),
                                preferred_element_type=jnp.float32)
    yield
    intra = {t: (scores[t] * decay_ref[t[0]]).astype(jnp.bfloat16) for t in tiles}
    yield
    o_intra = {t: _dot(intra[t], vs[t]) for t in tiles}
    yield
    o = {}
    for p in range(RET_HEADS // 2):
        state = state_ref[p]
        for n in range(n_chunks):
            qd = (q[p, n].astype(jnp.float32) * qdec_ref[p]).astype(jnp.bfloat16)
            o[p, n] = o_intra[p, n] + _dot(qd, state.astype(jnp.bfloat16))
            state = sdec_ref[p] * state + jnp.where(same_head, kv[p, n], 0.0)
        state_ref[p] = state
    for t in tiles:
        cen = o[t] - head_mean(o[t])
        var = head_mean(cen * cen)
        normed = cen * lax.rsqrt(var + NORM_EPS) * gain_ref[:, t[0] * LANES:(t[0] + 1) * LANES]
        gate = window(gate_ref, *t)
        out.store(t[1] * BLOCK, t[0] * LANES, gate * jax.nn.sigmoid(gate) * normed)
    yield


class _Columns:
    def __init__(self, ref, col0):
        self.ref, self.col0 = ref, col0

    def store(self, r0, c0, value):
        rows, cols = value.shape
        self.ref[r0:r0 + rows, self.col0 + c0:self.col0 + c0 + cols] = value.astype(self.ref.dtype)


FF_CHUNK = 512

STAGE_ROUNDS = {
    "mlp": (1, 2, 3, 4, 5, 6, 7, 8, 9, 10),
    "sb": (1, 2, 3, 4, 5) + (5, 6, 7, 8, 9),
    "swa": (7, 8, 9),
    "ret": (4, 5, 6, 7),
}


def _mlp_stages(mix_ref, x_ref, gpost_ref, gpre_ref, gmlp_ref, wout_ref, wup_ref, wdown_ref, o_ref):
    mixed = mix_ref[...]
    yield
    y = _dot(mixed, wout_ref[...])
    x1 = x_ref[...] + _rms(y, gpost_ref[...])
    hn = _rms(x1, gpre_ref[...]).astype(jnp.bfloat16)
    acc = jnp.zeros_like(x1)
    for f0 in range(0, D_FF, FF_CHUNK):
        u = jnp.maximum(_dot(hn, wup_ref[:, f0:f0 + FF_CHUNK]), 0.0)
        acc = acc + _dot((u * u).astype(jnp.bfloat16), wdown_ref[f0:f0 + FF_CHUNK, :])
        yield
    o_ref[...] = x1 + _rms(acc, gmlp_ref[...])
    yield


def _layer_kernel(sink_ref, head_ref, mid_ref, tail_ref, prev_ref, gate_ref, gain_ref,
                  decay_ref, qdec_ref, kdec_ref, sdec_ref,
                  x_ref, ga_ref, gb_ref, gpost_ref, gpre_ref, gmlp_ref, wout_ref, wup_ref, wdown_ref,
                  o_ref, state_ref, mix_ref, kb_all_ref, vb_all_ref, *, seq_tiles, n_tiles):
    s = pl.program_id(0)
    i = jnp.minimum(s, n_tiles - 1) % seq_tiles

    @pl.when(s == 0)
    def _():
        mix_ref[...] = jnp.zeros_like(mix_ref)

    @pl.when(i == 0)
    def _():
        state_ref[...] = jnp.zeros_like(state_ref)

    cols = lambda ref, blk0, blk, width: ref.at[:, :, (blk - blk0) * LANES:(blk - blk0) * LANES + width]
    qa_ref, qc_ref = cols(head_ref, QA_BLK, QA_BLK, SWA_Q_W), cols(head_ref, QA_BLK, QC_BLK, RET_W)
    kc_ref, vc_ref = cols(head_ref, QA_BLK, KC_BLK, RET_W), cols(head_ref, QA_BLK, VC_BLK, RET_W)
    ka_ref, va_ref = cols(mid_ref, KA_BLK, KA_BLK, LANES), cols(mid_ref, KA_BLK, VA_BLK, LANES)
    qb_ref = cols(mid_ref, KA_BLK, QB_BLK, SB_W)
    kb_ref, vb_ref = cols(tail_ref, KB_BLK, KB_BLK, SB_W), cols(tail_ref, KB_BLK, VB_BLK, SB_W)
    kap_ref, vap_ref = cols(prev_ref, KA_BLK, KA_BLK, LANES), cols(prev_ref, KA_BLK, VA_BLK, LANES)

    rows = pl.ds(pl.multiple_of(i * kb_ref.shape[1], kb_ref.shape[1]), kb_ref.shape[1])
    kb_all_ref[rows, :] = kb_ref[0]
    vb_all_ref[rows, :] = vb_ref[0]

    sb = _StickBreaking(i, qb_ref, kb_all_ref, vb_all_ref, gb_ref, _Columns(mix_ref, SWA_Q_W))
    stages = {
        "mlp": _mlp_stages(mix_ref, x_ref, gpost_ref, gpre_ref, gmlp_ref, wout_ref, wup_ref, wdown_ref, o_ref),
        "sb": itertools.chain(sb.fused_stages(0), sb.fused_stages(1)),
        "swa": _swa_stages(i, sink_ref, qa_ref, ka_ref, kap_ref, va_ref, vap_ref, ga_ref, _Columns(mix_ref, 0)),
        "ret": _ret_stages(qc_ref, kc_ref, vc_ref, gate_ref, gain_ref, decay_ref, qdec_ref, kdec_ref, sdec_ref,
                           _Columns(mix_ref, SWA_Q_W + SB_W), state_ref),
    }
    for rnd in range(1, 1 + max(max(r) for r in STAGE_ROUNDS.values())):
        for name, part in stages.items():
            for _ in range(STAGE_ROUNDS[name].count(rnd)):
                next(part)
    for part in stages.values():
        assert next(part, "done") == "done"
    sb.finish()


def _retention_tables():
    h = jnp.arange(RET_HEADS, dtype=jnp.float32)
    log_gamma = jnp.log1p(-(2.0 ** (-5.0 - h)))
    pos = jnp.arange(BLOCK, dtype=jnp.float32)
    rel = pos[:, None] - pos[None, :]
    decay = jnp.where(rel[None] >= 0,
                      jnp.exp(jnp.maximum(rel, 0.0)[None] * log_gamma[:, None, None]), 0.0)
    decay = jnp.swapaxes(decay.reshape(RET_HEADS // 2, 2, BLOCK, BLOCK), 1, 2)
    decay = decay.reshape(RET_HEADS // 2, BLOCK, 2 * BLOCK)
    per_lane = lambda t: jnp.repeat(t.reshape(-1, RET_HEADS // 2, 2), HEAD_DIM, axis=2)
    to_pairs = lambda t: jnp.moveaxis(per_lane(t).reshape(-1, RET_HEADS // 2, LANES), 1, 0)
    q_dec = to_pairs(jnp.exp((pos + 1.0)[:, None] * log_gamma[None, :]))
    k_dec = to_pairs(jnp.exp((BLOCK - 1 - pos)[:, None] * log_gamma[None, :]))
    chunk_decay = to_pairs(jnp.exp(BLOCK * log_gamma)[None, :])
    s_dec = jnp.broadcast_to(jnp.swapaxes(chunk_decay, 1, 2), (RET_HEADS // 2, LANES, LANES))
    return decay, q_dec, k_dec, s_dec


def _mix_mlp(sinks, proj, gate, gn_gain, x2d, ga, gb, g_post, g_pre, g_mlp, layer, w_out, w_up, w_down):
    b, s, _ = proj.shape
    tt = min(TOKEN_TILE, s)
    per = tt // BLOCK
    seq_tiles = s // tt
    n_tiles = b * seq_tiles
    tables = _retention_tables()
    mixer_tile = lambda g: jnp.minimum(g, n_tiles - 1)
    bi = lambda g: mixer_tile(g) // seq_tiles
    ti = lambda g: mixer_tile(g) % seq_tiles
    tile = lambda width, blk: pl.BlockSpec((1, tt, width), lambda g: (bi(g), ti(g), blk * LANES // width))
    prev = lambda width, blk: pl.BlockSpec(
        (1, BLOCK, width), lambda g: (bi(g), jnp.maximum(ti(g) * per - 1, 0), blk * LANES // width))
    const = lambda shape: pl.BlockSpec(shape, lambda g: (0,) * len(shape))
    weight = lambda shape: pl.BlockSpec((None,) + shape, lambda g: (layer, 0, 0), pipeline_mode=pl.Buffered(1))
    mlp_rows = pl.BlockSpec((tt, D_MODEL), lambda g: (jnp.maximum(g - 1, 0), 0))
    return pl.pallas_call(
        functools.partial(_layer_kernel, seq_tiles=seq_tiles, n_tiles=n_tiles),
        grid=(n_tiles + 1,),
        in_specs=[
            pl.BlockSpec(memory_space=pltpu.SMEM),
            tile(KA_BLK * LANES, QA_BLK), tile((KB_BLK - KA_BLK) * LANES, KA_BLK),
            tile(PROJ_W - KB_BLK * LANES, KB_BLK), prev(2 * LANES, KA_BLK),
            tile(RET_W, 0),
            const((1, RET_W)),
        ] + [const(t.shape) for t in tables] + [
            mlp_rows,
            const((1, SWA_Q_W)), const((1, SB_W)), const((1, D_MODEL)), const((1, D_MODEL)), const((1, D_MODEL)),
            weight((D_MODEL, D_MODEL)), weight((D_MODEL, D_FF)), weight((D_FF, D_MODEL)),
        ],
        out_specs=mlp_rows,
        out_shape=jax.ShapeDtypeStruct(x2d.shape, jnp.float32),
        scratch_shapes=[pltpu.VMEM((RET_HEADS // 2, LANES, LANES), jnp.float32),
                        pltpu.VMEM((tt, D_MODEL), jnp.bfloat16),
                        pltpu.VMEM((s, SB_W), jnp.bfloat16), pltpu.VMEM((s, SB_W), jnp.bfloat16)],
        compiler_params=pltpu.CompilerParams(
            dimension_semantics=("arbitrary",), vmem_limit_bytes=VMEM_LIMIT),
        name="mix_mlp",
    )(sinks, proj, proj, proj, proj, gate, gn_gain, *tables,
      x2d, ga, gb, g_post, g_pre, g_mlp, w_out, w_up, w_down)


def _rope_table(positions):
    inv_freq = ROPE_THETA ** (-jnp.arange(0, HEAD_DIM, 2, dtype=jnp.float32) / HEAD_DIM)
    ang = positions.astype(jnp.float32)[:, None] * inv_freq[None, :]
    cos = jnp.tile(jnp.cos(ang), (1, 4))
    sin = jnp.tile(jnp.concatenate([-jnp.sin(ang), jnp.sin(ang)], axis=1), (1, 2))
    return jnp.stack([cos, sin, cos * QK_SCALE, sin * QK_SCALE])


def _permute_swa_heads(t, axis):
    shape = t.shape
    t = t.reshape(shape[:axis] + (SWA_Q_HEADS, HEAD_DIM) + shape[axis + 1:])
    t = jnp.take(t, jnp.asarray(SWA_HEAD_ORDER), axis=axis)
    return t.reshape(shape)


def _reorder_w_in(w_in):
    edges = np.cumsum([0, SWA_Q_W, SWA_KV_W, SWA_KV_W, SB_W, SB_W, SB_W, RET_W, RET_W, RET_W, RET_W])
    qa, ka, va, qb, kb, vb, qc, kc, vc, gc = [w_in[..., a:e] for a, e in zip(edges[:-1], edges[1:])]
    return jnp.concatenate([_permute_swa_heads(qa, 2), qc, kc, vc, ka, va, qb, kb, vb, gc], axis=-1)


def kernel(x, positions, w_in, w_out, sinks, branch_gain, w_up, w_down,
           norm_mix_pre, norm_mix_post, norm_mlp_pre, norm_mlp_post):
    b, s, d = x.shape
    depth = w_in.shape[0]
    rope_tab = _rope_table(positions)
    w_in = _reorder_w_in(w_in).astype(jnp.bfloat16)
    w_out = jnp.concatenate([_permute_swa_heads(w_out[:, :SWA_Q_W], 1), w_out[:, SWA_Q_W:]],
                            axis=1).astype(jnp.bfloat16)
    gain_a = _permute_swa_heads(branch_gain[:, :SWA_Q_W], 1)
    w_up = w_up.astype(jnp.bfloat16)
    w_down = w_down.astype(jnp.bfloat16)
    row = lambda t: t.reshape(1, -1)

    x2d = x.reshape(b * s, d)
    for l in range(depth):
        proj, gate = _in_proj(x2d, row(norm_mix_pre[l]), w_in, l, rope_tab, s)
        x2d = _mix_mlp(sinks[l], proj.reshape(b, s, PROJ_W), gate.reshape(b, s, RET_W),
                       row(branch_gain[l, SWA_Q_W + SB_W:]), x2d,
                       row(gain_a[l]), row(branch_gain[l, SWA_Q_W:SWA_Q_W + SB_W]),
                       row(norm_mix_post[l]), row(norm_mlp_pre[l]), row(norm_mlp_post[l]),
                       l, w_out, w_up, w_down)
    return x2d.reshape(b, s, d)
```

```python
import functools
import itertools

import numpy as np
import jax
import jax.numpy as jnp
from jax import lax
from jax.experimental import pallas as pl
from jax.experimental.pallas import tpu as pltpu

D_MODEL = 1024
HEAD_DIM = 64
SWA_Q_HEADS = 6
SWA_KV_HEADS = 2
SB_HEADS = 4
RET_HEADS = 6
BLOCK = 128
ROPE_THETA = 10000.0
D_FF = 4 * D_MODEL
NORM_EPS = 1e-6
LANES = 128

SWA_Q_W = SWA_Q_HEADS * HEAD_DIM
SWA_KV_W = SWA_KV_HEADS * HEAD_DIM
SB_W = SB_HEADS * HEAD_DIM
RET_W = RET_HEADS * HEAD_DIM
IN_W = SWA_Q_W + 2 * SWA_KV_W + 3 * SB_W + 4 * RET_W
PROJ_W = IN_W - RET_W

QA_BLK = 0
QC_BLK, KC_BLK, VC_BLK = 3, 6, 9
KA_BLK, VA_BLK = 12, 13
QB_BLK, KB_BLK, VB_BLK = 14, 16, 18
GATE_BLK = 20
QK_SCALE = HEAD_DIM ** -0.5

SWA_HEAD_ORDER = (0, 3, 1, 4, 2, 5)

SB_DEAD_SUM = 104.0
SB_FUSED_BLOCKS = 3

TOKEN_TILE = 512
VMEM_LIMIT = 62 * 1024 * 1024


def _rms(xf, gain):
    return xf * lax.rsqrt(jnp.mean(xf * xf, axis=-1, keepdims=True) + NORM_EPS) * gain


def _low_half_mask(shape):
    return lax.broadcasted_iota(jnp.int32, shape, len(shape) - 1) < HEAD_DIM


def _stack_heads(pair):
    low = _low_half_mask(pair.shape)
    zero = jnp.zeros_like(pair)
    return jnp.concatenate([jnp.where(low, pair, zero), jnp.where(low, zero, pair)], axis=0)


def _dot_nt(a, b):
    return lax.dot_general(a, b, (((1,), (1,)), ((), ())), preferred_element_type=jnp.float32)


def _dot(a, b):
    return jnp.dot(a, b, preferred_element_type=jnp.float32)


def _in_blk_kind(blk):
    if QA_BLK <= blk < QC_BLK or KC_BLK <= blk < VC_BLK:
        return 1, 1.0
    if QC_BLK <= blk < KC_BLK or blk == KA_BLK:
        return 0, 1.0
    if QB_BLK <= blk < KB_BLK:
        return None, QK_SCALE
    return None, 1.0


def _in_proj_kernel(x_ref, g_ref, w_ref, rope_ref, proj_ref, gate_ref):
    hn = _rms(x_ref[...], g_ref[...]).astype(jnp.bfloat16)
    first_half = (lax.broadcasted_iota(jnp.int32, (x_ref.shape[0], LANES), 1) % HEAD_DIM) < HEAD_DIM // 2
    chunk = 2 * LANES
    for c0 in range(0, IN_W, chunk):
        width = min(chunk, IN_W - c0)
        p = _dot(hn, w_ref[:, c0:c0 + width])
        for s0 in range(0, width, LANES):
            blk = (c0 + s0) // LANES
            pb = p[:, s0:s0 + LANES]
            table, scale = _in_blk_kind(blk)
            if table is not None:
                cos = rope_ref[2 * table]
                sin = rope_ref[2 * table + 1]
                rot = jnp.where(first_half, pltpu.roll(pb, LANES - HEAD_DIM // 2, 1),
                                pltpu.roll(pb, HEAD_DIM // 2, 1))
                pb = pb * cos + rot * sin
            elif scale != 1.0:
                pb = pb * scale
            if blk >= GATE_BLK:
                g0 = (blk - GATE_BLK) * LANES
                gate_ref[:, g0:g0 + LANES] = pb
            else:
                proj_ref[:, blk * LANES:(blk + 1) * LANES] = pb.astype(jnp.bfloat16)


def _in_proj(x2d, gain, w_in, layer, rope_tab, seq):
    n_tok = x2d.shape[0]
    tm = min(TOKEN_TILE, seq)
    seq_tiles = seq // tm
    return pl.pallas_call(
        _in_proj_kernel,
        grid=(n_tok // tm,),
        in_specs=[
            pl.BlockSpec((tm, D_MODEL), lambda i: (i, 0)),
            pl.BlockSpec((1, D_MODEL), lambda i: (0, 0)),
            pl.BlockSpec((None, D_MODEL, IN_W), lambda i: (layer, 0, 0), pipeline_mode=pl.Buffered(1)),
            pl.BlockSpec((4, tm, LANES), lambda i: (0, i % seq_tiles, 0)),
        ],
        out_specs=[
            pl.BlockSpec((tm, PROJ_W), lambda i: (i, 0)),
            pl.BlockSpec((tm, RET_W), lambda i: (i, 0)),
        ],
        out_shape=[
            jax.ShapeDtypeStruct((n_tok, PROJ_W), jnp.bfloat16),
            jax.ShapeDtypeStruct((n_tok, RET_W), jnp.float32),
        ],
        compiler_params=pltpu.CompilerParams(
            dimension_semantics=("arbitrary",), vmem_limit_bytes=VMEM_LIMIT),
        name="in_proj",
    )(x2d, gain, w_in, rope_tab)


def _swa_stages(i, sink_ref, q_ref, k_ref, kp_ref, v_ref, vp_ref, gain_ref, out):
    n_blocks = q_ref.shape[1] // BLOCK
    rows = lax.broadcasted_iota(jnp.int32, (BLOCK, 2 * BLOCK), 0)
    cols = lax.broadcasted_iota(jnp.int32, (BLOCK, 2 * BLOCK), 1)
    own = (cols % BLOCK) <= rows
    low_lanes = _low_half_mask((BLOCK, LANES))
    first_bias = jnp.where(i > 0, 0.0, -jnp.inf).astype(jnp.float32)
    ks = [_stack_heads(kp_ref[0])] + [_stack_heads(k_ref[0, n * BLOCK:(n + 1) * BLOCK, :]) for n in range(n_blocks)]
    vs = [_stack_heads(vp_ref[0])] + [_stack_heads(v_ref[0, n * BLOCK:(n + 1) * BLOCK, :]) for n in range(n_blocks)]
    tiles = [(n, c) for n in range(n_blocks) for c in range(SWA_Q_HEADS // 2)]
    scores = {}
    for n, c in tiles:
        q = q_ref[0, n * BLOCK:(n + 1) * BLOCK, c * LANES:(c + 1) * LANES]
        s_prev = _dot_nt(q, ks[n])
        if n == 0:
            s_prev = s_prev + first_bias
        scores[n, c] = jnp.where(own, _dot_nt(q, ks[n + 1]), s_prev)
    yield
    weights = {}
    for n, c in tiles:
        probs, scales = [], []
        for half, head in enumerate(SWA_HEAD_ORDER[2 * c:2 * c + 2]):
            sh = scores[n, c][:, half * BLOCK:(half + 1) * BLOCK]
            sink = sink_ref[head]
            m = jnp.maximum(jnp.max(sh, axis=-1, keepdims=True), sink)
            p = jnp.exp(sh - m)
            probs.append(p)
            scales.append(1.0 / (jnp.sum(p, axis=-1, keepdims=True) + jnp.exp(sink - m)))
        p = jnp.concatenate(probs, axis=1)
        weights[n, c] = (jnp.where(own, p, 0.0).astype(jnp.bfloat16), jnp.where(own, 0.0, p).astype(jnp.bfloat16),
                         jnp.where(low_lanes, scales[0], scales[1]))
    yield
    for n in range(n_blocks):
        heads = []
        for c in range(SWA_Q_HEADS // 2):
            p_own, p_prev, scale = weights[n, c]
            heads.append((_dot(p_own, vs[n + 1]) + _dot(p_prev, vs[n])) * scale)
        out.store(n * BLOCK, 0, _rms(jnp.concatenate(heads, axis=1), gain_ref[...]))
    yield


class _StickBreaking:
    def __init__(self, i, q_ref, k_ref, v_ref, gain_ref, out):
        self.k_ref, self.v_ref, self.gain_ref, self.out = k_ref, v_ref, gain_ref, out
        self.subs = q_ref.shape[1] // BLOCK
        self.first = i * self.subs
        pairs = SB_HEADS // 2
        wide = 2 * BLOCK
        self.units = [(u, p) for u in range(self.subs) for p in range(pairs)]
        self.q = {(u, p): q_ref[0, u * BLOCK:(u + 1) * BLOCK, p * LANES:(p + 1) * LANES] for u, p in self.units}
        rows = lax.broadcasted_iota(jnp.int32, (BLOCK, wide), 0)
        cols = lax.broadcasted_iota(jnp.int32, (BLOCK, wide), 1)
        self.strict = (cols % BLOCK) < rows
        kj = lax.broadcasted_iota(jnp.int32, (wide, wide), 0)
        ks = lax.broadcasted_iota(jnp.int32, (wide, wide), 1)
        self.tail_mat = jnp.where(((kj < BLOCK) == (ks < BLOCK)) & (kj > ks), 1.0, 0.0).astype(jnp.bfloat16)
        self.carry = {key: (jnp.zeros((BLOCK, 1), jnp.float32),) * 2 for key in self.units}
        self.acc = {key: jnp.zeros((BLOCK, LANES), jnp.float32) for key in self.units}

    def _load_values(self, p, j):
        start = pl.multiple_of(j * BLOCK, BLOCK)
        return _stack_heads(self.v_ref[pl.ds(start, BLOCK), p * LANES:(p + 1) * LANES])

    def _load_keys(self, p, j):
        kt = self.k_ref[j, p * LANES:(p + 1) * LANES, :]
        low = lax.broadcasted_iota(jnp.int32, kt.shape, 0) < HEAD_DIM
        zero = jnp.zeros_like(kt)
        return jnp.concatenate([jnp.where(low, kt, zero), jnp.where(low, zero, kt)], axis=1)

    @staticmethod
    def _bias_if(valid):
        return jnp.where(valid, 0.0, -jnp.inf).astype(jnp.float32)

    def _tiles(self, units, jobs, carry, acc, out):
        keys, vals = {}, {}
        for u, p in units:
            for j, _ in jobs[u]:
                if (p, id(j)) not in keys:
                    keys[p, id(j)], vals[p, id(j)] = self._load_keys(p, j), self._load_values(p, j)
        scores = {(u, p, t): _dot(self.q[u, p], keys[p, id(j)])
                  for u, p in units for t, (j, _) in enumerate(jobs[u])}
        yield
        log_beta, tails, row_sums = {}, {}, {}
        for key, z in scores.items():
            _, mask = jobs[key[0]][key[2]]
            if isinstance(mask, str):
                z = jnp.where(self.strict, z, -jnp.inf)
            elif mask is not None:
                z = z + mask
            sp = jnp.maximum(z, 0.0) + jnp.log(1.0 + jnp.exp(-jnp.abs(z)))
            log_beta[key] = z - sp
            tails[key] = sp.astype(jnp.bfloat16)
            row_sums[key] = (jnp.sum(sp[:, :BLOCK], axis=-1, keepdims=True),
                             jnp.sum(sp[:, BLOCK:], axis=-1, keepdims=True))
        yield
        tails = {key: _dot(sp, self.tail_mat) for key, sp in tails.items()}
        yield
        new_carry, weights = {}, {}
        for u, p in units:
            c = carry[u, p]
            ws = []
            for t in range(len(jobs[u])):
                key = (u, p, t)
                logits = log_beta[key] - tails[key]
                ws += [jnp.exp(logits[:, h * BLOCK:(h + 1) * BLOCK] - c[h]).astype(jnp.bfloat16) for h in range(2)]
                c = tuple(c[h] + row_sums[key][h] for h in range(2))
            new_carry[u, p] = c
            weights[u, p] = jnp.concatenate(ws, axis=1)
        yield
        new_acc = {}
        for u, p in units:
            v_cat = jnp.concatenate([vals[p, id(j)] for j, _ in jobs[u]], axis=0)
            new_acc[u, p] = acc[u, p] + _dot(weights[u, p], v_cat)
        out["carry"], out["acc"] = new_carry, new_acc
        yield

    def fused_stages(self, wave):
        first, subs = self.first, self.subs
        block_at = {d: jnp.maximum(first + d, 0) for d in range(1 - SB_FUSED_BLOCKS, subs)}
        jobs = [[(block_at[u - back],
                  "diag" if back == 0 else (None if u >= back else self._bias_if(first + u - back >= 0)))
                 for back in range(SB_FUSED_BLOCKS)] for u in range(subs)]
        half = len(self.units) // 2
        out = {}
        yield from self._tiles(self.units[wave * half:(wave + 1) * half], jobs, self.carry, self.acc, out)
        self.carry.update(out["carry"])
        self.acc.update(out["acc"])

    def finish(self):
        first, subs, units = self.first, self.subs, self.units

        def alive_after(cs):
            least = functools.reduce(jnp.minimum, [c for key in units for c in cs[key]])
            return (jnp.min(least) < SB_DEAD_SUM).astype(jnp.int32)

        flat = lambda d: tuple(d[key] for key in units)
        unflat = lambda t: dict(zip(units, t))

        def cond(state):
            back, alive, _, _ = state
            return jnp.logical_and(first + subs - 1 - back >= 0, alive > 0)

        def body(state):
            back, _, cs, accs = state
            jobs = [[(jnp.maximum(first + u - back, 0),
                      None if u == subs - 1 else self._bias_if(first + u - back >= 0))] for u in range(subs)]
            out = {}
            for _ in self._tiles(units, jobs, unflat(cs), unflat(accs), out):
                pass
            return back + 1, alive_after(out["carry"]), flat(out["carry"]), flat(out["acc"])

        state = (jnp.int32(SB_FUSED_BLOCKS), alive_after(self.carry), flat(self.carry), flat(self.acc))
        acc = unflat(lax.while_loop(cond, body, state)[3])
        for u in range(subs):
            heads = jnp.concatenate([acc[u, p] for p in range(SB_HEADS // 2)], axis=1)
            self.out.store(u * BLOCK, 0, _rms(heads, self.gain_ref[...]))


def _ret_stages(q_ref, k_ref, v_ref, gate_ref, gain_ref, decay_ref, qdec_ref, kdec_ref, sdec_ref,
                out, state_ref):
    n_chunks = q_ref.shape[1] // BLOCK
    low = _low_half_mask((BLOCK, LANES))
    same_head = (lax.broadcasted_iota(jnp.int32, (LANES, LANES), 0) < HEAD_DIM) == _low_half_mask((LANES, LANES))

    def head_mean(t):
        lo_sum = jnp.sum(jnp.where(low, t, 0.0), axis=-1, keepdims=True)
        hi_sum = jnp.sum(jnp.where(low, 0.0, t), axis=-1, keepdims=True)
        return jnp.where(low, lo_sum, hi_sum) * (1.0 / HEAD_DIM)

    tiles = [(p, n) for p in range(RET_HEADS // 2) for n in range(n_chunks)]
    window = lambda ref, p, n: ref[0, n * BLOCK:(n + 1) * BLOCK, p * LANES:(p + 1) * LANES]
    q = {t: window(q_ref, *t) for t in tiles}
    k = {t: window(k_ref, *t) for t in tiles}
    vs = {t: _stack_heads(window(v_ref, *t)) for t in tiles}
    scores = {t: _dot_nt(q[t], _stack_heads(k[t])) for t in tiles}
    kv = {}
    for t in tiles:
        kd = (k[t].astype(jnp.float32) * kdec_ref[t[0]]).astype(jnp.bfloat16)
        kv[t] = lax.dot_general(kd, window(v_ref, *t), (((0,), (0,)), ((), ())),
                                preferred_element_type=jnp.float32)
    yield
    intra = {t: (scores[t] * decay_ref[t[0]]).astype(jnp.bfloat16) for t in tiles}
    yield
    o_intra = {t: _dot(intra[t], vs[t]) for t in tiles}
    yield
    o = {}
    for p in range(RET_HEADS // 2):
        state = state_ref[p]
        for n in range(n_chunks):
            qd = (q[p, n].astype(jnp.float32) * qdec_ref[p]).astype(jnp.bfloat16)
            o[p, n] = o_intra[p, n] + _dot(qd, state.astype(jnp.bfloat16))
            state = sdec_ref[p] * state + jnp.where(same_head, kv[p, n], 0.0)
        state_ref[p] = state
    for t in tiles:
        cen = o[t] - head_mean(o[t])
        var = head_mean(cen * cen)
        normed = cen * lax.rsqrt(var + NORM_EPS) * gain_ref[:, t[0] * LANES:(t[0] + 1) * LANES]
        gate = window(gate_ref, *t)
        out.store(t[1] * BLOCK, t[0] * LANES, gate * jax.nn.sigmoid(gate) * normed)
    yield


class _Columns:
    def __init__(self, ref, col0):
        self.ref, self.col0 = ref, col0

    def store(self, r0, c0, value):
        rows, cols = value.shape
        self.ref[r0:r0 + rows, self.col0 + c0:self.col0 + c0 + cols] = value.astype(self.ref.dtype)


FF_CHUNK = 512

STAGE_ROUNDS = {
    "mlp": (1, 2, 3, 4, 5, 6, 7, 8, 9, 10),
    "sb": (1, 2, 3, 4, 5) + (5, 6, 7, 8, 9),
    "swa": (7, 8, 9),
    "ret": (4, 5, 6, 7),
}


def _mlp_stages(mix_ref, x_ref, gpost_ref, gpre_ref, gmlp_ref, wout_ref, wup_ref, wdown_ref, o_ref):
    mixed = mix_ref[...]
    yield
    y = _dot(mixed, wout_ref[...])
    x1 = x_ref[...] + _rms(y, gpost_ref[...])
    hn = _rms(x1, gpre_ref[...]).astype(jnp.bfloat16)
    acc = jnp.zeros_like(x1)
    for f0 in range(0, D_FF, FF_CHUNK):
        u = jnp.maximum(_dot(hn, wup_ref[:, f0:f0 + FF_CHUNK]), 0.0)
        acc = acc + _dot((u * u).astype(jnp.bfloat16), wdown_ref[f0:f0 + FF_CHUNK, :])
        yield
    o_ref[...] = x1 + _rms(acc, gmlp_ref[...])
    yield


def _layer_kernel(sink_ref, qa_ref, ka_ref, kap_ref, va_ref, vap_ref, qb_ref, kb_ref, vb_ref,
                  qc_ref, kc_ref, vc_ref, gate_ref, gain_ref, decay_ref, qdec_ref, kdec_ref, sdec_ref,
                  x_ref, ga_ref, gb_ref, gpost_ref, gpre_ref, gmlp_ref, wout_ref, wup_ref, wdown_ref,
                  o_ref, state_ref, mix_ref, kb_all_ref, vb_all_ref, *, seq_tiles, n_tiles):
    s = pl.program_id(0)
    i = jnp.minimum(s, n_tiles - 1) % seq_tiles

    @pl.when(s == 0)
    def _():
        mix_ref[...] = jnp.zeros_like(mix_ref)

    @pl.when(i == 0)
    def _():
        state_ref[...] = jnp.zeros_like(state_ref)

    rows = pl.ds(pl.multiple_of(i * kb_ref.shape[1], kb_ref.shape[1]), kb_ref.shape[1])
    vb_all_ref[rows, :] = vb_ref[0]
    keys_t = kb_ref[0].astype(jnp.float32).T.astype(jnp.bfloat16)
    for n in range(kb_ref.shape[1] // BLOCK):
        kb_all_ref[i * (kb_ref.shape[1] // BLOCK) + n] = keys_t[:, n * BLOCK:(n + 1) * BLOCK]

    sb = _StickBreaking(i, qb_ref, kb_all_ref, vb_all_ref, gb_ref, _Columns(mix_ref, SWA_Q_W))
    stages = {
        "mlp": _mlp_stages(mix_ref, x_ref, gpost_ref, gpre_ref, gmlp_ref, wout_ref, wup_ref, wdown_ref, o_ref),
        "sb": itertools.chain(sb.fused_stages(0), sb.fused_stages(1)),
        "swa": _swa_stages(i, sink_ref, qa_ref, ka_ref, kap_ref, va_ref, vap_ref, ga_ref, _Columns(mix_ref, 0)),
        "ret": _ret_stages(qc_ref, kc_ref, vc_ref, gate_ref, gain_ref, decay_ref, qdec_ref, kdec_ref, sdec_ref,
                           _Columns(mix_ref, SWA_Q_W + SB_W), state_ref),
    }
    for rnd in range(1, 1 + max(max(r) for r in STAGE_ROUNDS.values())):
        for name, part in stages.items():
            for _ in range(STAGE_ROUNDS[name].count(rnd)):
                next(part)
    for part in stages.values():
        assert next(part, "done") == "done"
    sb.finish()


def _retention_tables():
    h = jnp.arange(RET_HEADS, dtype=jnp.float32)
    log_gamma = jnp.log1p(-(2.0 ** (-5.0 - h)))
    pos = jnp.arange(BLOCK, dtype=jnp.float32)
    rel = pos[:, None] - pos[None, :]
    decay = jnp.where(rel[None] >= 0,
                      jnp.exp(jnp.maximum(rel, 0.0)[None] * log_gamma[:, None, None]), 0.0)
    decay = jnp.swapaxes(decay.reshape(RET_HEADS // 2, 2, BLOCK, BLOCK), 1, 2)
    decay = decay.reshape(RET_HEADS // 2, BLOCK, 2 * BLOCK)
    per_lane = lambda t: jnp.repeat(t.reshape(-1, RET_HEADS // 2, 2), HEAD_DIM, axis=2)
    to_pairs = lambda t: jnp.moveaxis(per_lane(t).reshape(-1, RET_HEADS // 2, LANES), 1, 0)
    q_dec = to_pairs(jnp.exp((pos + 1.0)[:, None] * log_gamma[None, :]))
    k_dec = to_pairs(jnp.exp((BLOCK - 1 - pos)[:, None] * log_gamma[None, :]))
    chunk_decay = to_pairs(jnp.exp(BLOCK * log_gamma)[None, :])
    s_dec = jnp.broadcast_to(jnp.swapaxes(chunk_decay, 1, 2), (RET_HEADS // 2, LANES, LANES))
    return decay, q_dec, k_dec, s_dec


def _mix_mlp(sinks, proj, gate, gn_gain, x2d, ga, gb, g_post, g_pre, g_mlp, layer, w_out, w_up, w_down):
    b, s, _ = proj.shape
    tt = min(TOKEN_TILE, s)
    per = tt // BLOCK
    seq_tiles = s // tt
    n_tiles = b * seq_tiles
    tables = _retention_tables()
    mixer_tile = lambda g: jnp.minimum(g, n_tiles - 1)
    bi = lambda g: mixer_tile(g) // seq_tiles
    ti = lambda g: mixer_tile(g) % seq_tiles
    tile = lambda width, blk: pl.BlockSpec((1, tt, width), lambda g: (bi(g), ti(g), blk * LANES // width))
    prev = lambda blk: pl.BlockSpec((1, BLOCK, LANES), lambda g: (bi(g), jnp.maximum(ti(g) * per - 1, 0), blk))
    const = lambda shape: pl.BlockSpec(shape, lambda g: (0,) * len(shape))
    weight = lambda shape: pl.BlockSpec((None,) + shape, lambda g: (layer, 0, 0), pipeline_mode=pl.Buffered(1))
    mlp_rows = pl.BlockSpec((tt, D_MODEL), lambda g: (jnp.maximum(g - 1, 0), 0))
    return pl.pallas_call(
        functools.partial(_layer_kernel, seq_tiles=seq_tiles, n_tiles=n_tiles),
        grid=(n_tiles + 1,),
        in_specs=[
            pl.BlockSpec(memory_space=pltpu.SMEM),
            tile(SWA_Q_W, QA_BLK), tile(LANES, KA_BLK), prev(KA_BLK), tile(LANES, VA_BLK), prev(VA_BLK),
            tile(SB_W, QB_BLK), tile(SB_W, KB_BLK), tile(SB_W, VB_BLK),
            tile(RET_W, QC_BLK), tile(RET_W, KC_BLK), tile(RET_W, VC_BLK), tile(RET_W, 0),
            const((1, RET_W)),
        ] + [const(t.shape) for t in tables] + [
            mlp_rows,
            const((1, SWA_Q_W)), const((1, SB_W)), const((1, D_MODEL)), const((1, D_MODEL)), const((1, D_MODEL)),
            weight((D_MODEL, D_MODEL)), weight((D_MODEL, D_FF)), weight((D_FF, D_MODEL)),
        ],
        out_specs=mlp_rows,
        out_shape=jax.ShapeDtypeStruct(x2d.shape, jnp.float32),
        scratch_shapes=[pltpu.VMEM((RET_HEADS // 2, LANES, LANES), jnp.float32),
                        pltpu.VMEM((tt, D_MODEL), jnp.bfloat16),
                        pltpu.VMEM((s // BLOCK, SB_W, BLOCK), jnp.bfloat16), pltpu.VMEM((s, SB_W), jnp.bfloat16)],
        compiler_params=pltpu.CompilerParams(
            dimension_semantics=("arbitrary",), vmem_limit_bytes=VMEM_LIMIT),
        name="mix_mlp",
    )(sinks, proj, proj, proj, proj, proj, proj, proj, proj, proj, proj, proj, gate, gn_gain, *tables,
      x2d, ga, gb, g_post, g_pre, g_mlp, w_out, w_up, w_down)


def _rope_table(positions):
    inv_freq = ROPE_THETA ** (-jnp.arange(0, HEAD_DIM, 2, dtype=jnp.float32) / HEAD_DIM)
    ang = positions.astype(jnp.float32)[:, None] * inv_freq[None, :]
    cos = jnp.tile(jnp.cos(ang), (1, 4))
    sin = jnp.tile(jnp.concatenate([-jnp.sin(ang), jnp.sin(ang)], axis=1), (1, 2))
    return jnp.stack([cos, sin, cos * QK_SCALE, sin * QK_SCALE])


def _permute_swa_heads(t, axis):
    shape = t.shape
    t = t.reshape(shape[:axis] + (SWA_Q_HEADS, HEAD_DIM) + shape[axis + 1:])
    t = jnp.take(t, jnp.asarray(SWA_HEAD_ORDER), axis=axis)
    return t.reshape(shape)


def _reorder_w_in(w_in):
    edges = np.cumsum([0, SWA_Q_W, SWA_KV_W, SWA_KV_W, SB_W, SB_W, SB_W, RET_W, RET_W, RET_W, RET_W])
    qa, ka, va, qb, kb, vb, qc, kc, vc, gc = [w_in[..., a:e] for a, e in zip(edges[:-1], edges[1:])]
    return jnp.concatenate([_permute_swa_heads(qa, 2), qc, kc, vc, ka, va, qb, kb, vb, gc], axis=-1)


def kernel(x, positions, w_in, w_out, sinks, branch_gain, w_up, w_down,
           norm_mix_pre, norm_mix_post, norm_mlp_pre, norm_mlp_post):
    b, s, d = x.shape
    depth = w_in.shape[0]
    rope_tab = _rope_table(positions)
    w_in = _reorder_w_in(w_in).astype(jnp.bfloat16)
    w_out = jnp.concatenate([_permute_swa_heads(w_out[:, :SWA_Q_W], 1), w_out[:, SWA_Q_W:]],
                            axis=1).astype(jnp.bfloat16)
    gain_a = _permute_swa_heads(branch_gain[:, :SWA_Q_W], 1)
    w_up = w_up.astype(jnp.bfloat16)
    w_down = w_down.astype(jnp.bfloat16)
    row = lambda t: t.reshape(1, -1)

    x2d = x.reshape(b * s, d)
    for l in range(depth):
        proj, gate = _in_proj(x2d, row(norm_mix_pre[l]), w_in, l, rope_tab, s)
        x2d = _mix_mlp(sinks[l], proj.reshape(b, s, PROJ_W), gate.reshape(b, s, RET_W),
                       row(branch_gain[l, SWA_Q_W + SB_W:]), x2d,
                       row(gain_a[l]), row(branch_gain[l, SWA_Q_W:SWA_Q_W + SB_W]),
                       row(norm_mix_post[l]), row(norm_mlp_pre[l]), row(norm_mlp_post[l]),
                       l, w_out, w_up, w_down)
    return x2d.reshape(b, s, d)
```
